```python
import jax, jax.numpy as jnp
from jax import lax
import numpy as np

D_MODEL = 1024
BATCH = 2
SEQ = 8192
DEPTH = 1

D_MIX = D_MODEL
D_CONV = D_MIX // 2
D_GMLP = D_MIX - D_CONV
N_CONV_GROUPS = 8
N_GMLP_HEADS = 8
GMLP_HEAD_DIM = D_GMLP // N_GMLP_HEADS
CONV_WIDTH = 31
CHUNK = 128
N_PROJ = 2 * D_CONV + 2 * D_GMLP
N_KEYS = 128
N_EXPERTS = N_KEYS * N_KEYS
PEER_HEADS = 8
D_KEY = 256
D_HALF = D_KEY // 2
TOPK = 16
TOKEN_BLOCK = 128
N_COND = 6
EPS = 1e-6

kernel_name = "hybrid_conv_gmlp_peer_adaln"


def rms_norm(x, g):
    xf = x.astype(jnp.float32)
    y = xf * lax.rsqrt(jnp.mean(xf * xf, axis=-1, keepdims=True) + EPS)
    return (y * g.astype(jnp.float32)).astype(x.dtype)


def group_layer_norm(x, n_groups, g, b):
    shp = x.shape
    xf = x.astype(jnp.float32).reshape(shp[:-1] + (n_groups, shp[-1] // n_groups))
    mu = jnp.mean(xf, axis=-1, keepdims=True)
    var = jnp.mean(jnp.square(xf - mu), axis=-1, keepdims=True)
    y = ((xf - mu) * lax.rsqrt(var + EPS)).reshape(shp)
    return (y * g.astype(jnp.float32) + b.astype(jnp.float32)).astype(x.dtype)


def conformer_conv_groups(a_val, a_gate, conv_w, conv_b, g_ln, b_ln):
    a = a_val * jax.nn.sigmoid(a_gate)
    y = lax.conv_general_dilated(
        a, conv_w[:, None, :].astype(a.dtype), window_strides=(1,),
        padding=[(CONV_WIDTH - 1, 0)],
        dimension_numbers=('NWC', 'WIO', 'NWC'),
        feature_group_count=D_CONV) + conv_b
    y = group_layer_norm(y, N_CONV_GROUPS, g_ln, b_ln)
    return jax.nn.silu(y)


def gmlp_groups(z, w_s, b_s, g_ln, b_ln, mask):
    B, S, _ = z.shape
    z = jax.nn.gelu(z)
    u, v = jnp.split(z, 2, axis=-1)
    v = group_layer_norm(v, N_GMLP_HEADS, g_ln, b_ln)
    v = v.reshape(B, S // CHUNK, CHUNK, N_GMLP_HEADS, GMLP_HEAD_DIM)
    mixed = jnp.einsum('hts,bnshd->bnthd', w_s * mask, v)
    mixed = mixed + jnp.transpose(b_s)[None, None, :, :, None]
    return u * mixed.reshape(B, S, D_GMLP)


def peer_ffn(h, w_q, sub_keys, expert_u, expert_v):
    B, S, D = h.shape
    T = B * S
    hf = h.reshape(T, D)
    q = (hf @ w_q).reshape(T, PEER_HEADS, 2, D_HALF)
    s = jnp.einsum('thid,hind->thin', q, sub_keys)
    sv, si = lax.top_k(s, TOPK)
    cand = (sv[:, :, 0, :, None] + sv[:, :, 1, None, :]).reshape(T, PEER_HEADS, TOPK * TOPK)
    cidx = (si[:, :, 0, :, None] * N_KEYS + si[:, :, 1, None, :]).reshape(T, PEER_HEADS, TOPK * TOPK)
    top_s, pos = lax.top_k(cand, TOPK)
    eidx = jnp.take_along_axis(cidx, pos, axis=-1)
    gates = jax.nn.softmax(top_s.astype(jnp.float32), axis=-1).astype(h.dtype)
    nb = T // TOKEN_BLOCK

    def apply_block(args):
        hb, eb, gb = args
        u = expert_u[eb]
        act = jax.nn.gelu(jnp.einsum('thkd,td->thk', u, hb)) * gb
        return jnp.einsum('thk,thkd->td', act, expert_v[eb])

    y = lax.map(apply_block, (hf.reshape(nb, TOKEN_BLOCK, D),
                              eidx.reshape(nb, TOKEN_BLOCK, PEER_HEADS, TOPK),
                              gates.reshape(nb, TOKEN_BLOCK, PEER_HEADS, TOPK)))
    return y.reshape(B, S, D)


def setup_inputs(seed: int = 0) -> dict:
    key = jax.random.key(seed)
    ks = jax.random.split(key, 24)
    n = jax.random.normal
    L, D = DEPTH, D_MODEL
    return {
        "x": n(ks[0], (BATCH, SEQ, D), jnp.float32),
        "c": n(ks[1], (BATCH, D), jnp.float32),
        "w_ada": n(ks[2], (L, D, N_COND * D), jnp.float32) * (0.5 * D ** -0.5),
        "b_ada": n(ks[3], (L, N_COND * D), jnp.float32) * 0.02,
        "g_norm1": 1.0 + 0.02 * n(ks[4], (L, D), jnp.float32),
        "w_in": n(ks[5], (L, D, N_PROJ), jnp.float32) * D ** -0.5,
        "b_in": n(ks[6], (L, N_PROJ), jnp.float32) * 0.02,
        "conv_w": n(ks[7], (L, CONV_WIDTH, D_CONV), jnp.float32) * CONV_WIDTH ** -0.5,
        "conv_b": n(ks[8], (L, D_CONV), jnp.float32) * 0.02,
        "g_conv_ln": 1.0 + 0.02 * n(ks[9], (L, D_CONV), jnp.float32),
        "b_conv_ln": n(ks[10], (L, D_CONV), jnp.float32) * 0.02,
        "g_v_ln": 1.0 + 0.02 * n(ks[11], (L, D_GMLP), jnp.float32),
        "b_v_ln": n(ks[12], (L, D_GMLP), jnp.float32) * 0.02,
        "w_spatial": n(ks[13], (L, N_GMLP_HEADS, CHUNK, CHUNK), jnp.float32) * CHUNK ** -0.5,
        "b_spatial": 1.0 + 0.02 * n(ks[14], (L, N_GMLP_HEADS, CHUNK), jnp.float32),
        "w_out": n(ks[15], (L, D_MIX, D), jnp.float32) * D_MIX ** -0.5,
        "b_out": n(ks[16], (L, D), jnp.float32) * 0.02,
        "g_norm2": 1.0 + 0.02 * n(ks[17], (L, D), jnp.float32),
        "w_query": n(ks[18], (L, D, PEER_HEADS * D_KEY), jnp.float32) * D ** -0.5,
        "sub_keys": n(ks[19], (L, PEER_HEADS, 2, N_KEYS, D_HALF), jnp.float32) * D_HALF ** -0.5,
        "expert_u": n(ks[20], (L, N_EXPERTS, D), jnp.float32) * D ** -0.5,
        "expert_v": n(ks[21], (L, N_EXPERTS, D), jnp.float32) * PEER_HEADS ** -0.5,
        "g_final": 1.0 + 0.02 * n(ks[22], (D,), jnp.float32),
    }


def reference(x, c, w_ada, b_ada, g_norm1, w_in, b_in, conv_w, conv_b, g_conv_ln, b_conv_ln,
              g_v_ln, b_v_ln, w_spatial, b_spatial, w_out, b_out, g_norm2, w_query, sub_keys,
              expert_u, expert_v, g_final):
    mask = jnp.tril(jnp.ones((CHUNK, CHUNK), dtype=w_spatial.dtype))
    for l in range(DEPTH):
        cond = jax.nn.silu(c) @ w_ada[l] + b_ada[l]
        sh1, sc1, gt1, sh2, sc2, gt2 = jnp.split(cond[:, None, :], N_COND, axis=-1)
        h = rms_norm(x, g_norm1[l]) * (1.0 + sc1) + sh1
        p = h @ w_in[l] + b_in[l]
        a_val, a_gate, z = jnp.split(p, [D_CONV, 2 * D_CONV], axis=-1)
        ya = conformer_conv_groups(a_val, a_gate, conv_w[l], conv_b[l], g_conv_ln[l], b_conv_ln[l])
        yb = gmlp_groups(z, w_spatial[l], b_spatial[l], g_v_ln[l], b_v_ln[l], mask)
        y = jnp.concatenate([ya, yb], axis=-1) @ w_out[l] + b_out[l]
        x = x + gt1 * y
        h2 = rms_norm(x, g_norm2[l]) * (1.0 + sc2) + sh2
        x = x + gt2 * peer_ffn(h2, w_query[l], sub_keys[l], expert_u[l], expert_v[l])
    return rms_norm(x, g_final)
```

```python
import functools

import jax
import jax.numpy as jnp
from jax import lax
from jax.experimental import pallas as pl
from jax.experimental.pallas import tpu as pltpu

F32 = jnp.float32
BF16 = jnp.bfloat16

D_MODEL = 1024
D_CONV = 512
D_GMLP = 512
GROUP = 64
N_HEADS_G = 8
CONV_WIDTH = 31
CHUNK = 128
N_PROJ = 2048
N_KEYS = 128
N_EXPERTS = N_KEYS * N_KEYS
PEER_HEADS = 8
D_HALF = 128
TOPK = 16
N_COND = 6
EPS = 1e-6

V7X_VMEM_BYTES = 64 * 1024 * 1024
VMEM_LIMIT = V7X_VMEM_BYTES * 3 // 4

TS = 512
CONV_HALO = 32
CONV_ROWS = 64
TB = 512
EB = 1024
LANE_CHUNK = 128


def _rms(x):
    return x * lax.rsqrt(jnp.mean(x * x, axis=-1, keepdims=True) + EPS)


def _split_dot(x, m):
    hi = x.astype(BF16)
    lo = (x - hi.astype(F32)).astype(BF16)
    return (jnp.dot(hi, m, preferred_element_type=F32)
            + jnp.dot(lo, m, preferred_element_type=F32))


def _group_ln(y, mg, g, b):
    mu = _split_dot(y, mg)
    d = y - mu
    var = _split_dot(d * d, mg)
    return d * lax.rsqrt(var + EPS) * g + b


def _cond_kernel(c_ref, w_ref, b_ref, o_ref):
    c = c_ref[...]
    s = c * jax.nn.sigmoid(c)
    o_ref[...] = jnp.dot(s, w_ref[...], preferred_element_type=F32,
                         precision=lax.Precision.HIGHEST) + b_ref[...]


def _cond_call(c_pad, w_ada, b_ada):
    n = w_ada.shape[1]
    blk = 1024
    return pl.pallas_call(
        _cond_kernel,
        grid=(n // blk,),
        in_specs=[pl.BlockSpec((8, D_MODEL), lambda j: (0, 0)),
                  pl.BlockSpec((D_MODEL, blk), lambda j: (0, j)),
                  pl.BlockSpec((1, blk), lambda j: (0, j))],
        out_specs=pl.BlockSpec((8, blk), lambda j: (0, j)),
        out_shape=jax.ShapeDtypeStruct((8, n), F32),
        compiler_params=pltpu.CompilerParams(
            dimension_semantics=("arbitrary",), vmem_limit_bytes=VMEM_LIMIT),
        name="cond",
    )(c_pad, w_ada, b_ada)


def _mixer_kernel(x_ref, cond_ref, g1_ref, win_ref, bin_ref, cw_ref, cb_ref,
                  gcl_ref, bcl_ref, gvl_ref, bvl_ref, ws_ref, bs_ref, mg_ref,
                  wout_ref, bout_ref, x1_ref, abuf, ybuf):
    s = pl.program_id(1)
    x = x_ref[0]
    cond = cond_ref[0]
    sh1, sc1, gt1 = cond[0:1], cond[1:2], cond[2:3]
    h = _rms(x) * g1_ref[...] * (1.0 + sc1) + sh1
    p = jnp.dot(h.astype(BF16), win_ref[...], preferred_element_type=F32) + bin_ref[...]

    a = p[:, :D_CONV] * jax.nn.sigmoid(p[:, D_CONV:2 * D_CONV])

    @pl.when(s == 0)
    def _():
        abuf[0:CONV_HALO, :] = jnp.zeros((CONV_HALO, D_CONV), F32)

    @pl.when(s > 0)
    def _():
        abuf[0:CONV_HALO, :] = abuf[TS:TS + CONV_HALO, :]

    abuf[CONV_HALO:CONV_HALO + TS, :] = a

    off = CONV_HALO - (CONV_WIDTH - 1)

    for r in range(TS // CONV_ROWS):
        base = r * CONV_ROWS
        acc = jnp.zeros((CONV_ROWS, D_CONV), F32) + cb_ref[...]
        for k in range(CONV_WIDTH):
            acc = acc + abuf[base + off + k:base + off + k + CONV_ROWS, :] * cw_ref[k:k + 1, :]
        ybuf[base:base + CONV_ROWS, :] = acc
    mg = mg_ref[...]
    ya = _group_ln(ybuf[...], mg, gcl_ref[...], bcl_ref[...])
    ya = ya * jax.nn.sigmoid(ya)

    z = jax.nn.gelu(p[:, 2 * D_CONV:])
    u = z[:, :D_GMLP]
    v = _group_ln(z[:, D_GMLP:], mg, gvl_ref[...], bvl_ref[...])
    row = lax.broadcasted_iota(jnp.int32, (CHUNK, CHUNK), 0)
    col = lax.broadcasted_iota(jnp.int32, (CHUNK, CHUNK), 1)
    lane_head = lax.broadcasted_iota(jnp.int32, (CHUNK, D_GMLP), 1) // GROUP
    w_heads = [jnp.where(row >= col, ws_ref[hh], 0.0).astype(BF16) for hh in range(N_HEADS_G)]
    yb_chunks = []
    for ch in range(TS // CHUNK):
        vc = v[ch * CHUNK:(ch + 1) * CHUNK, :]
        mixed = bs_ref[...]
        for hh in range(N_HEADS_G):
            vm = jnp.where(lane_head == hh, vc, 0.0).astype(BF16)
            mixed = mixed + jnp.dot(w_heads[hh], vm, preferred_element_type=F32)
        yb_chunks.append(u[ch * CHUNK:(ch + 1) * CHUNK, :] * mixed)
    yb = jnp.concatenate(yb_chunks, axis=0)

    y = (jnp.dot(ya.astype(BF16), wout_ref[0:D_CONV, :], preferred_element_type=F32)
         + jnp.dot(yb.astype(BF16), wout_ref[D_CONV:, :], preferred_element_type=F32)
         + bout_ref[...])
    x1_ref[0] = x + gt1 * y


def _mixer_call(x, cond3, g1, win, bin_, cw, cb, gcl, bcl, gvl, bvl, ws, bs, mg, wout, bout):
    B, S, D = x.shape
    full = lambda shape: pl.BlockSpec(shape, lambda b, s: (0,) * len(shape))
    return pl.pallas_call(
        _mixer_kernel,
        grid=(B, S // TS),
        in_specs=[pl.BlockSpec((1, TS, D), lambda b, s: (b, s, 0)),
                  pl.BlockSpec((1, N_COND, D), lambda b, s: (b, 0, 0)),
                  full((1, D)), full((D, N_PROJ)), full((1, N_PROJ)),
                  full((CONV_WIDTH, D_CONV)), full((1, D_CONV)),
                  full((1, D_CONV)), full((1, D_CONV)), full((1, D_GMLP)), full((1, D_GMLP)),
                  full((N_HEADS_G, CHUNK, CHUNK)), full((CHUNK, D_GMLP)),
                  full((D_CONV, D_CONV)), full((D, D)), full((1, D))],
        out_specs=pl.BlockSpec((1, TS, D), lambda b, s: (b, s, 0)),
        out_shape=jax.ShapeDtypeStruct((B, S, D), F32),
        scratch_shapes=[pltpu.VMEM((TS + CONV_HALO, D_CONV), F32),
                        pltpu.VMEM((TS, D_CONV), F32)],
        compiler_params=pltpu.CompilerParams(
            dimension_semantics=("arbitrary", "arbitrary"), vmem_limit_bytes=VMEM_LIMIT),
        name="mixer",
    )(x, cond3, g1, win, bin_, cw, cb, gcl, bcl, gvl, bvl, ws, bs, mg, wout, bout)


def _top_values(work, n):
    vals = []
    for _ in range(n):
        m = jnp.max(work, axis=0, keepdims=True)
        vals.append(m)
        work = jnp.where(work == m, -jnp.inf, work)
    return vals


def _prep_kernel(x1_ref, cond_ref, g2_ref, wq_ref, keys_ref,
                 h2t_ref, s1_ref, e1_ref, thr_ref, e0_ref, q_scr):
    cond = cond_ref[0]
    sh2, sc2 = cond[3:4], cond[4:5]
    h2 = _rms(x1_ref[...]) * g2_ref[...] * (1.0 + sc2) + sh2
    h2t_ref[...] = h2.T.astype(BF16)
    q_scr[...] = jnp.dot(h2.astype(BF16), wq_ref[...], preferred_element_type=F32)

    def head(hh, carry):
        c0 = pl.multiple_of(hh * (2 * D_HALF), 2 * D_HALF)
        q0 = q_scr[:, pl.ds(c0, D_HALF)].astype(BF16)
        q1 = q_scr[:, pl.ds(c0 + D_HALF, D_HALF)].astype(BF16)
        nt = (((1,), (1,)), ((), ()))
        s0 = lax.dot_general(keys_ref[2 * hh], q0, nt, preferred_element_type=F32)
        s1 = lax.dot_general(keys_ref[2 * hh + 1], q1, nt, preferred_element_type=F32)
        a = _top_values(s0, TOPK)
        b = _top_values(s1, TOPK)
        bmat = jnp.concatenate(b, axis=0)
        cand = jnp.concatenate([a[r] + bmat for r in range(TOPK)], axis=0)
        top = _top_values(cand, TOPK)
        tau = top[TOPK - 1]
        zsum = jnp.zeros_like(tau)
        for t in top:
            zsum = zsum + jnp.exp(t - top[0])
        thr = jnp.full(s0.shape, jnp.inf, F32)
        for c in range(TOPK):
            thr = jnp.where(s0 + b[c] >= tau, b[c], thr)
        s1_ref[hh] = s1
        e1_ref[hh] = jnp.exp(s1 - b[0]) / zsum
        thr_ref[hh] = thr
        e0_ref[hh] = jnp.exp(s0 - a[0])
        return carry

    lax.fori_loop(0, PEER_HEADS, head, 0)


def _prep_call(x1, cond3, g2, wq, keys, tiles_per_batch):
    T, D = x1.shape
    tok = lambda: pl.BlockSpec((PEER_HEADS, N_KEYS, TB), lambda i: (0, 0, i))
    sds = jax.ShapeDtypeStruct((PEER_HEADS, N_KEYS, T), F32)
    return pl.pallas_call(
        _prep_kernel,
        grid=(T // TB,),
        in_specs=[pl.BlockSpec((TB, D), lambda i: (i, 0)),
                  pl.BlockSpec((1, N_COND, D), lambda i: (i // tiles_per_batch, 0, 0)),
                  pl.BlockSpec((1, D), lambda i: (0, 0)),
                  pl.BlockSpec((D, 2 * PEER_HEADS * D_HALF), lambda i: (0, 0)),
                  pl.BlockSpec((2 * PEER_HEADS, N_KEYS, D_HALF), lambda i: (0, 0, 0))],
        out_specs=[pl.BlockSpec((D, TB), lambda i: (0, i)), tok(), tok(), tok(), tok()],
        out_shape=[jax.ShapeDtypeStruct((D, T), BF16), sds, sds, sds, sds],
        scratch_shapes=[pltpu.VMEM((TB, 2 * PEER_HEADS * D_HALF), F32)],
        compiler_params=pltpu.CompilerParams(
            dimension_semantics=("arbitrary",), vmem_limit_bytes=VMEM_LIMIT),
        name="prep",
    )(x1, cond3, g2, wq, keys)


def _peer_kernel(h2t_ref, s1_ref, e1_ref, thr_ref, e0_ref, u_ref, vt_ref,
                 x1_ref, cond_ref, gf_ref, o_ref, a_scr, z_scr, acc_scr):
    j = pl.program_id(1)
    blocks = EB // N_KEYS

    @pl.when(j == 0)
    def _():
        acc_scr[...] = jnp.zeros_like(acc_scr)

    a_scr[...] = jnp.dot(u_ref[...], h2t_ref[...], preferred_element_type=F32)

    for ii in range(blocks):

        def lanes(lc, carry, ii=ii):
            l0 = pl.multiple_of(lc * LANE_CHUNK, LANE_CHUNK)
            g = jnp.zeros((N_KEYS, LANE_CHUNK), F32)
            for hh in range(PEER_HEADS):
                thr = thr_ref[hh, ii:ii + 1, pl.ds(l0, LANE_CHUNK)]
                e0 = e0_ref[hh, ii:ii + 1, pl.ds(l0, LANE_CHUNK)]
                s1 = s1_ref[hh, :, pl.ds(l0, LANE_CHUNK)]
                e1 = e1_ref[hh, :, pl.ds(l0, LANE_CHUNK)]
                g = g + jnp.where(s1 >= thr, e1, 0.0) * e0
            act = jax.nn.gelu(a_scr[ii * N_KEYS:(ii + 1) * N_KEYS, pl.ds(l0, LANE_CHUNK)])
            z_scr[ii * N_KEYS:(ii + 1) * N_KEYS, pl.ds(l0, LANE_CHUNK)] = (act * g).astype(BF16)
            return carry

        lax.fori_loop(0, TB // LANE_CHUNK, lanes, 0)

    acc_scr[...] += jnp.dot(vt_ref[...], z_scr[...], preferred_element_type=F32)

    @pl.when(j == pl.num_programs(1) - 1)
    def _():
        gt2 = cond_ref[0][5:6]
        x2 = x1_ref[...] + gt2 * acc_scr[...].T
        o_ref[...] = _rms(x2) * gf_ref[...]


def _peer_call(h2t, s1, e1, thr, e0, u_bf, vt_bf, x1, cond3, gf, tiles_per_batch):
    T, D = x1.shape
    tok = lambda: pl.BlockSpec((PEER_HEADS, N_KEYS, TB), lambda i, j: (0, 0, i))
    rows = lambda: pl.BlockSpec((PEER_HEADS, EB // N_KEYS, TB), lambda i, j: (0, j, i))
    return pl.pallas_call(
        _peer_kernel,
        grid=(T // TB, N_EXPERTS // EB),
        in_specs=[pl.BlockSpec((D, TB), lambda i, j: (0, i)),
                  tok(), tok(), rows(), rows(),
                  pl.BlockSpec((EB, D), lambda i, j: (j, 0)),
                  pl.BlockSpec((D, EB), lambda i, j: (0, j)),
                  pl.BlockSpec((TB, D), lambda i, j: (i, 0)),
                  pl.BlockSpec((1, N_COND, D), lambda i, j: (i // tiles_per_batch, 0, 0)),
                  pl.BlockSpec((1, D), lambda i, j: (0, 0))],
        out_specs=pl.BlockSpec((TB, D), lambda i, j: (i, 0)),
        out_shape=jax.ShapeDtypeStruct((T, D), F32),
        scratch_shapes=[pltpu.VMEM((EB, TB), F32),
                        pltpu.VMEM((EB, TB), BF16),
                        pltpu.VMEM((D, TB), F32)],
        compiler_params=pltpu.CompilerParams(
            dimension_semantics=("arbitrary", "arbitrary"), vmem_limit_bytes=VMEM_LIMIT),
        name="peer",
    )(h2t, s1, e1, thr, e0, u_bf, vt_bf, x1, cond3, gf)


def kernel(x, c, w_ada, b_ada, g_norm1, w_in, b_in, conv_w, conv_b, g_conv_ln, b_conv_ln,
           g_v_ln, b_v_ln, w_spatial, b_spatial, w_out, b_out, g_norm2, w_query, sub_keys,
           expert_u, expert_v, g_final):
    B, S, D = x.shape
    depth = w_ada.shape[0]
    assert depth == 1, "the final rmsnorm is fused into the PEER call of the only layer"
    T = B * S
    row = lambda v: v.reshape(1, -1)
    grp = jnp.arange(D_CONV) // GROUP
    mg = jnp.where(grp[:, None] == grp[None, :], 1.0 / GROUP, 0.0).astype(BF16)
    c_pad = jnp.pad(c, ((0, 8 - B), (0, 0)))
    for l in range(depth):
        cond = _cond_call(c_pad, w_ada[l], row(b_ada[l]))
        cond3 = cond[:B].reshape(B, N_COND, D)
        bs = jnp.repeat(b_spatial[l].T, GROUP, axis=1)
        x1 = _mixer_call(x, cond3, row(g_norm1[l]), w_in[l].astype(BF16), row(b_in[l]),
                         conv_w[l], row(conv_b[l]), row(g_conv_ln[l]), row(b_conv_ln[l]),
                         row(g_v_ln[l]), row(b_v_ln[l]), w_spatial[l], bs, mg,
                         w_out[l].astype(BF16), row(b_out[l]))
        x1 = x1.reshape(T, D)
        keys = sub_keys[l].reshape(2 * PEER_HEADS, N_KEYS, D_HALF).astype(BF16)
        h2t, s1, e1, thr, e0 = _prep_call(x1, cond3, row(g_norm2[l]), w_query[l].astype(BF16),
                                          keys, S // TB)
        x = _peer_call(h2t, s1, e1, thr, e0, expert_u[l].astype(BF16),
                       expert_v[l].T.astype(BF16), x1, cond3, row(g_final),
                       S // TB).reshape(B, S, D)
    return x
```

```python
import functools

import jax
import jax.numpy as jnp
from jax import lax
from jax.experimental import pallas as pl
from jax.experimental.pallas import tpu as pltpu

F32 = jnp.float32
BF16 = jnp.bfloat16

D_MODEL = 1024
D_CONV = 512
D_GMLP = 512
GROUP = 64
N_HEADS_G = 8
CONV_WIDTH = 31
CHUNK = 128
N_PROJ = 2048
N_KEYS = 128
N_EXPERTS = N_KEYS * N_KEYS
PEER_HEADS = 8
D_HALF = 128
TOPK = 16
N_COND = 6
EPS = 1e-6

V7X_VMEM_BYTES = 64 * 1024 * 1024
VMEM_LIMIT = V7X_VMEM_BYTES * 3 // 4

TS = 512
CONV_HALO = 32
CONV_ROWS = 64
TB = 512
EB = 1024
STAGE = 1024
LANES = 128
BF16_ROWS = 16
GATE_IB = 2


def _rms(x):
    return x * lax.rsqrt(jnp.mean(x * x, axis=-1, keepdims=True) + EPS)


def _split_dot(x, m):
    hi = x.astype(BF16)
    lo = (x - hi.astype(F32)).astype(BF16)
    return (jnp.dot(hi, m, preferred_element_type=F32)
            + jnp.dot(lo, m, preferred_element_type=F32))


def _group_ln(y, mg, g, b):
    mu = _split_dot(y, mg)
    d = y - mu
    var = _split_dot(d * d, mg)
    return d * lax.rsqrt(var + EPS) * g + b


def _cond_kernel(c_ref, w_ref, b_ref, o_ref):
    c = c_ref[...]
    s = c * jax.nn.sigmoid(c)
    o_ref[...] = jnp.dot(s, w_ref[...], preferred_element_type=F32,
                         precision=lax.Precision.HIGHEST) + b_ref[...]


def _cond_call(c_pad, w_ada, b_ada):
    n = w_ada.shape[1]
    blk = 1024
    return pl.pallas_call(
        _cond_kernel,
        grid=(n // blk,),
        in_specs=[pl.BlockSpec((8, D_MODEL), lambda j: (0, 0)),
                  pl.BlockSpec((D_MODEL, blk), lambda j: (0, j)),
                  pl.BlockSpec((1, blk), lambda j: (0, j))],
        out_specs=pl.BlockSpec((8, blk), lambda j: (0, j)),
        out_shape=jax.ShapeDtypeStruct((8, n), F32),
        compiler_params=pltpu.CompilerParams(
            dimension_semantics=("arbitrary",), vmem_limit_bytes=VMEM_LIMIT),
        name="cond",
    )(c_pad, w_ada, b_ada)


def _mixer_kernel(x_ref, cond_ref, g1_ref, win_ref, bin_ref, cw_ref, cb_ref,
                  gcl_ref, bcl_ref, gvl_ref, bvl_ref, ws_ref, bs_ref, mg_ref,
                  wout_ref, bout_ref, x1_ref, abuf, ybuf):
    s = pl.program_id(1)
    x = x_ref[0]
    cond = cond_ref[0]
    sh1, sc1, gt1 = cond[0:1], cond[1:2], cond[2:3]
    h = _rms(x) * g1_ref[...] * (1.0 + sc1) + sh1
    p = jnp.dot(h.astype(BF16), win_ref[...], preferred_element_type=F32) + bin_ref[...]

    a = p[:, :D_CONV] * jax.nn.sigmoid(p[:, D_CONV:2 * D_CONV])

    @pl.when(s == 0)
    def _():
        abuf[0:CONV_HALO, :] = jnp.zeros((CONV_HALO, D_CONV), F32)

    @pl.when(s > 0)
    def _():
        abuf[0:CONV_HALO, :] = abuf[TS:TS + CONV_HALO, :]

    abuf[CONV_HALO:CONV_HALO + TS, :] = a

    off = CONV_HALO - (CONV_WIDTH - 1)

    for r in range(TS // CONV_ROWS):
        base = r * CONV_ROWS
        acc = jnp.zeros((CONV_ROWS, D_CONV), F32) + cb_ref[...]
        for k in range(CONV_WIDTH):
            acc = acc + abuf[base + off + k:base + off + k + CONV_ROWS, :] * cw_ref[k:k + 1, :]
        ybuf[base:base + CONV_ROWS, :] = acc
    mg = mg_ref[...]
    ya = _group_ln(ybuf[...], mg, gcl_ref[...], bcl_ref[...])
    ya = ya * jax.nn.sigmoid(ya)

    z = jax.nn.gelu(p[:, 2 * D_CONV:])
    u = z[:, :D_GMLP]
    v = _group_ln(z[:, D_GMLP:], mg, gvl_ref[...], bvl_ref[...])
    row = lax.broadcasted_iota(jnp.int32, (CHUNK, CHUNK), 0)
    col = lax.broadcasted_iota(jnp.int32, (CHUNK, CHUNK), 1)
    lane_head = lax.broadcasted_iota(jnp.int32, (CHUNK, D_GMLP), 1) // GROUP
    w_heads = [jnp.where(row >= col, ws_ref[hh], 0.0).astype(BF16) for hh in range(N_HEADS_G)]
    yb_chunks = []
    for ch in range(TS // CHUNK):
        vc = v[ch * CHUNK:(ch + 1) * CHUNK, :]
        mixed = bs_ref[...]
        for hh in range(N_HEADS_G):
            vm = jnp.where(lane_head == hh, vc, 0.0).astype(BF16)
            mixed = mixed + jnp.dot(w_heads[hh], vm, preferred_element_type=F32)
        yb_chunks.append(u[ch * CHUNK:(ch + 1) * CHUNK, :] * mixed)
    yb = jnp.concatenate(yb_chunks, axis=0)

    y = (jnp.dot(ya.astype(BF16), wout_ref[0:D_CONV, :], preferred_element_type=F32)
         + jnp.dot(yb.astype(BF16), wout_ref[D_CONV:, :], preferred_element_type=F32)
         + bout_ref[...])
    x1_ref[0] = x + gt1 * y


def _mixer_call(x, cond3, g1, win, bin_, cw, cb, gcl, bcl, gvl, bvl, ws, bs, mg, wout, bout):
    B, S, D = x.shape
    full = lambda shape: pl.BlockSpec(shape, lambda b, s: (0,) * len(shape))
    return pl.pallas_call(
        _mixer_kernel,
        grid=(B, S // TS),
        in_specs=[pl.BlockSpec((1, TS, D), lambda b, s: (b, s, 0)),
                  pl.BlockSpec((1, N_COND, D), lambda b, s: (b, 0, 0)),
                  full((1, D)), full((D, N_PROJ)), full((1, N_PROJ)),
                  full((CONV_WIDTH, D_CONV)), full((1, D_CONV)),
                  full((1, D_CONV)), full((1, D_CONV)), full((1, D_GMLP)), full((1, D_GMLP)),
                  full((N_HEADS_G, CHUNK, CHUNK)), full((CHUNK, D_GMLP)),
                  full((D_CONV, D_CONV)), full((D, D)), full((1, D))],
        out_specs=pl.BlockSpec((1, TS, D), lambda b, s: (b, s, 0)),
        out_shape=jax.ShapeDtypeStruct((B, S, D), F32),
        scratch_shapes=[pltpu.VMEM((TS + CONV_HALO, D_CONV), F32),
                        pltpu.VMEM((TS, D_CONV), F32)],
        compiler_params=pltpu.CompilerParams(
            dimension_semantics=("arbitrary", "arbitrary"), vmem_limit_bytes=VMEM_LIMIT),
        name="mixer",
    )(x, cond3, g1, win, bin_, cw, cb, gcl, bcl, gvl, bvl, ws, bs, mg, wout, bout)


def _top_values(work, n, with_rank=False):
    vals = []
    rank = jnp.full(work.shape, float(n), F32)
    for c in range(n):
        m = jnp.max(work, axis=0, keepdims=True)
        vals.append(m)
        hit = work == m
        if with_rank:
            rank = jnp.where(hit, float(c), rank)
        work = jnp.where(hit, -jnp.inf, work)
    return (vals, rank) if with_rank else vals


def _prep_kernel(x1_ref, cond_ref, g2_ref, wq_ref, keys_ref,
                 h2t_ref, rk_ref, e1_ref, cnt_ref, e0_ref, q_scr):
    cond = cond_ref[0]
    sh2, sc2 = cond[3:4], cond[4:5]
    h2 = _rms(x1_ref[...]) * g2_ref[...] * (1.0 + sc2) + sh2
    h2t_ref[...] = h2.T.astype(BF16)
    q_scr[...] = jnp.dot(h2.astype(BF16), wq_ref[...], preferred_element_type=F32)

    def head(hh, carry):
        c0 = pl.multiple_of(hh * (2 * D_HALF), 2 * D_HALF)
        q0 = q_scr[:, pl.ds(c0, D_HALF)].astype(BF16)
        q1 = q_scr[:, pl.ds(c0 + D_HALF, D_HALF)].astype(BF16)
        nt = (((1,), (1,)), ((), ()))
        s0 = lax.dot_general(keys_ref[2 * hh], q0, nt, preferred_element_type=F32)
        s1 = lax.dot_general(keys_ref[2 * hh + 1], q1, nt, preferred_element_type=F32)
        a = _top_values(s0, TOPK)
        b, rank1 = _top_values(s1, TOPK, with_rank=True)
        bmat = jnp.concatenate(b, axis=0)
        cand = jnp.concatenate([a[r] + bmat for r in range(TOPK)], axis=0)
        top = _top_values(cand, TOPK)
        tau = top[TOPK - 1]
        zsum = jnp.zeros_like(tau)
        for t in top:
            zsum = zsum + jnp.exp(t - top[0])
        count = jnp.zeros(s0.shape, F32)
        for c in range(TOPK):
            count = count + jnp.where(s0 + b[c] >= tau, 1.0, 0.0)
        rk_ref[hh] = rank1.astype(BF16)
        e1_ref[hh] = (jnp.exp(s1 - b[0]) / zsum).astype(BF16)
        cnt_ref[hh] = count
        e0_ref[hh] = jnp.exp(s0 - a[0])
        return carry

    lax.fori_loop(0, PEER_HEADS, head, 0)


def _prep_call(x1, cond3, g2, wq, keys, tiles_per_batch):
    T, D = x1.shape
    tok = lambda: pl.BlockSpec((PEER_HEADS, N_KEYS, TB), lambda i: (0, 0, i))
    sds = lambda dt: jax.ShapeDtypeStruct((PEER_HEADS, N_KEYS, T), dt)
    return pl.pallas_call(
        _prep_kernel,
        grid=(T // TB,),
        in_specs=[pl.BlockSpec((TB, D), lambda i: (i, 0)),
                  pl.BlockSpec((1, N_COND, D), lambda i: (i // tiles_per_batch, 0, 0)),
                  pl.BlockSpec((1, D), lambda i: (0, 0)),
                  pl.BlockSpec((D, 2 * PEER_HEADS * D_HALF), lambda i: (0, 0)),
                  pl.BlockSpec((2 * PEER_HEADS, N_KEYS, D_HALF), lambda i: (0, 0, 0))],
        out_specs=[pl.BlockSpec((D, TB), lambda i: (0, i)), tok(), tok(), tok(), tok()],
        out_shape=[jax.ShapeDtypeStruct((D, T), BF16),
                   sds(BF16), sds(BF16), sds(F32), sds(F32)],
        scratch_shapes=[pltpu.VMEM((TB, 2 * PEER_HEADS * D_HALF), F32)],
        compiler_params=pltpu.CompilerParams(
            dimension_semantics=("arbitrary",), vmem_limit_bytes=VMEM_LIMIT),
        name="prep",
    )(x1, cond3, g2, wq, keys)


def _peer_kernel(h2t_ref, rk_ref, e1_ref, cnt_ref, e0_ref, u_ref, vt_ref,
                 x1_ref, cond_ref, gf_ref, o_ref, rk_scr, e1_scr, a_scr, z_scr, acc_scr):
    j = pl.program_id(1)

    @pl.when(j == 0)
    def _():
        acc_scr[...] = jnp.zeros_like(acc_scr)
        rk_scr[...] = rk_ref[...]
        e1_scr[...] = e1_ref[...]

    zero = jnp.zeros((BF16_ROWS, LANES), BF16)
    jtiles = N_KEYS // BF16_ROWS
    for st in range(EB // STAGE):
        erows = slice(st * STAGE, (st + 1) * STAGE)
        a_scr[st] = jnp.dot(u_ref[erows, :], h2t_ref[...], preferred_element_type=F32)
        for ib in range(STAGE // N_KEYS // GATE_IB):
            for lc in range(TB // LANES):
                lanes = slice(lc * LANES, (lc + 1) * LANES)
                g = [[None] * jtiles for _ in range(GATE_IB)]
                for hh in range(PEER_HEADS):
                    rk = [rk_scr[hh, q * BF16_ROWS:(q + 1) * BF16_ROWS, lanes]
                          for q in range(jtiles)]
                    e1 = [e1_scr[hh, q * BF16_ROWS:(q + 1) * BF16_ROWS, lanes]
                          for q in range(jtiles)]
                    for p in range(GATE_IB):
                        ii = st * (STAGE // N_KEYS) + ib * GATE_IB + p
                        cnt = jnp.broadcast_to(cnt_ref[hh, ii:ii + 1, lanes],
                                               (BF16_ROWS, LANES)).astype(BF16)
                        e0 = jnp.broadcast_to(e0_ref[hh, ii:ii + 1, lanes],
                                              (BF16_ROWS, LANES)).astype(BF16)
                        for q in range(jtiles):
                            term = jnp.where(rk[q] < cnt, e1[q], zero) * e0
                            g[p][q] = term if g[p][q] is None else g[p][q] + term
                for p in range(GATE_IB):
                    for q in range(jtiles):
                        r0 = (ib * GATE_IB + p) * N_KEYS + q * BF16_ROWS
                        act = jax.nn.gelu(a_scr[st, r0:r0 + BF16_ROWS, lanes].astype(BF16))
                        z_scr[st, r0:r0 + BF16_ROWS, lanes] = act * g[p][q]
        acc_scr[...] += jnp.dot(vt_ref[:, erows], z_scr[st], preferred_element_type=F32)

    @pl.when(j == pl.num_programs(1) - 1)
    def _():
        gt2 = cond_ref[0][5:6]
        x2 = x1_ref[...] + gt2 * acc_scr[...].T
        o_ref[...] = _rms(x2) * gf_ref[...]


def _peer_call(h2t, rk, e1, cnt, e0, u_bf, vt_bf, x1, cond3, gf, tiles_per_batch):
    T, D = x1.shape
    tok = lambda: pl.BlockSpec((PEER_HEADS, N_KEYS, TB), lambda i, j: (0, 0, i))
    rows = lambda: pl.BlockSpec((PEER_HEADS, EB // N_KEYS, TB), lambda i, j: (0, j, i))
    return pl.pallas_call(
        _peer_kernel,
        grid=(T // TB, N_EXPERTS // EB),
        in_specs=[pl.BlockSpec((D, TB), lambda i, j: (0, i)),
                  tok(), tok(), rows(), rows(),
                  pl.BlockSpec((EB, D), lambda i, j: (j, 0)),
                  pl.BlockSpec((D, EB), lambda i, j: (0, j)),
                  pl.BlockSpec((TB, D), lambda i, j: (i, 0)),
                  pl.BlockSpec((1, N_COND, D), lambda i, j: (i // tiles_per_batch, 0, 0)),
                  pl.BlockSpec((1, D), lambda i, j: (0, 0))],
        out_specs=pl.BlockSpec((TB, D), lambda i, j: (i, 0)),
        out_shape=jax.ShapeDtypeStruct((T, D), F32),
        scratch_shapes=[pltpu.VMEM((PEER_HEADS, N_KEYS, TB), BF16),
                        pltpu.VMEM((PEER_HEADS, N_KEYS, TB), BF16),
                        pltpu.VMEM((EB // STAGE, STAGE, TB), F32),
                        pltpu.VMEM((EB // STAGE, STAGE, TB), BF16),
                        pltpu.VMEM((D, TB), F32)],
        compiler_params=pltpu.CompilerParams(
            dimension_semantics=("arbitrary", "arbitrary"), vmem_limit_bytes=VMEM_LIMIT),
        name="peer",
    )(h2t, rk, e1, cnt, e0, u_bf, vt_bf, x1, cond3, gf)


def kernel(x, c, w_ada, b_ada, g_norm1, w_in, b_in, conv_w, conv_b, g_conv_ln, b_conv_ln,
           g_v_ln, b_v_ln, w_spatial, b_spatial, w_out, b_out, g_norm2, w_query, sub_keys,
           expert_u, expert_v, g_final):
    B, S, D = x.shape
    depth = w_ada.shape[0]
    assert depth == 1, "the final rmsnorm is fused into the PEER call of the only layer"
    T = B * S
    row = lambda v: v.reshape(1, -1)
    grp = jnp.arange(D_CONV) // GROUP
    mg = jnp.where(grp[:, None] == grp[None, :], 1.0 / GROUP, 0.0).astype(BF16)
    c_pad = jnp.pad(c, ((0, 8 - B), (0, 0)))
    for l in range(depth):
        cond = _cond_call(c_pad, w_ada[l], row(b_ada[l]))
        cond3 = cond[:B].reshape(B, N_COND, D)
        bs = jnp.repeat(b_spatial[l].T, GROUP, axis=1)
        x1 = _mixer_call(x, cond3, row(g_norm1[l]), w_in[l].astype(BF16), row(b_in[l]),
                         conv_w[l], row(conv_b[l]), row(g_conv_ln[l]), row(b_conv_ln[l]),
                         row(g_v_ln[l]), row(b_v_ln[l]), w_spatial[l], bs, mg,
                         w_out[l].astype(BF16), row(b_out[l]))
        x1 = x1.reshape(T, D)
        keys = sub_keys[l].reshape(2 * PEER_HEADS, N_KEYS, D_HALF).astype(BF16)
        h2t, rk, e1, cnt, e0 = _prep_call(x1, cond3, row(g_norm2[l]), w_query[l].astype(BF16),
                                          keys, S // TB)
        x = _peer_call(h2t, rk, e1, cnt, e0, expert_u[l].astype(BF16),
                       expert_v[l].T.astype(BF16), x1, cond3, row(g_final),
                       S // TB).reshape(B, S, D)
    return x
```

```python
import functools

import jax
import jax.numpy as jnp
from jax import lax
from jax.experimental import pallas as pl
from jax.experimental.pallas import tpu as pltpu

F32 = jnp.float32
BF16 = jnp.bfloat16

D_MODEL = 1024
D_CONV = 512
D_GMLP = 512
GROUP = 64
N_HEADS_G = 8
CONV_WIDTH = 31
CHUNK = 128
N_PROJ = 2048
N_KEYS = 128
N_EXPERTS = N_KEYS * N_KEYS
PEER_HEADS = 8
D_HALF = 128
TOPK = 16
N_COND = 6
EPS = 1e-6

V7X_VMEM_BYTES = 64 * 1024 * 1024
VMEM_LIMIT = V7X_VMEM_BYTES * 3 // 4

TS = 512
CONV_HALO = 32
CONV_ROWS = 64
TB = 512
EB = 1024
STAGE = 1024
LANES = 128
BF16_ROWS = 16
GATE_IB = 2


def _rms(x):
    return x * lax.rsqrt(jnp.mean(x * x, axis=-1, keepdims=True) + EPS)


def _split_dot(x, m):
    hi = x.astype(BF16)
    lo = (x - hi.astype(F32)).astype(BF16)
    return (jnp.dot(hi, m, preferred_element_type=F32)
            + jnp.dot(lo, m, preferred_element_type=F32))


def _group_ln(y, mg, g, b):
    mu = _split_dot(y, mg)
    d = y - mu
    var = _split_dot(d * d, mg)
    return d * lax.rsqrt(var + EPS) * g + b


def _cond_kernel(c_ref, w_ref, b_ref, o_ref):
    c = c_ref[...]
    s = c * jax.nn.sigmoid(c)
    o_ref[...] = jnp.dot(s, w_ref[...], preferred_element_type=F32,
                         precision=lax.Precision.HIGHEST) + b_ref[...]


def _cond_call(c_pad, w_ada, b_ada):
    n = w_ada.shape[1]
    blk = 1024
    return pl.pallas_call(
        _cond_kernel,
        grid=(n // blk,),
        in_specs=[pl.BlockSpec((8, D_MODEL), lambda j: (0, 0)),
                  pl.BlockSpec((D_MODEL, blk), lambda j: (0, j)),
                  pl.BlockSpec((1, blk), lambda j: (0, j))],
        out_specs=pl.BlockSpec((8, blk), lambda j: (0, j)),
        out_shape=jax.ShapeDtypeStruct((8, n), F32),
        compiler_params=pltpu.CompilerParams(
            dimension_semantics=("arbitrary",), vmem_limit_bytes=VMEM_LIMIT),
        name="cond",
    )(c_pad, w_ada, b_ada)


def _mixer_kernel(x_ref, cond_ref, g1_ref, win_ref, bin_ref, cw_ref, cb_ref,
                  gcl_ref, bcl_ref, gvl_ref, bvl_ref, ws_ref, bs_ref, mg_ref,
                  wout_ref, bout_ref, x1_ref, abuf, ybuf):
    s = pl.program_id(1)
    x = x_ref[0]
    cond = cond_ref[0]
    sh1, sc1, gt1 = cond[0:1], cond[1:2], cond[2:3]
    h = _rms(x) * g1_ref[...] * (1.0 + sc1) + sh1
    p = jnp.dot(h.astype(BF16), win_ref[...], preferred_element_type=F32) + bin_ref[...]

    a = p[:, :D_CONV] * jax.nn.sigmoid(p[:, D_CONV:2 * D_CONV])

    @pl.when(s == 0)
    def _():
        abuf[0:CONV_HALO, :] = jnp.zeros((CONV_HALO, D_CONV), F32)

    @pl.when(s > 0)
    def _():
        abuf[0:CONV_HALO, :] = abuf[TS:TS + CONV_HALO, :]

    abuf[CONV_HALO:CONV_HALO + TS, :] = a

    off = CONV_HALO - (CONV_WIDTH - 1)

    for r in range(TS // CONV_ROWS):
        base = r * CONV_ROWS
        acc = jnp.zeros((CONV_ROWS, D_CONV), F32) + cb_ref[...]
        for k in range(CONV_WIDTH):
            acc = acc + abuf[base + off + k:base + off + k + CONV_ROWS, :] * cw_ref[k:k + 1, :]
        ybuf[base:base + CONV_ROWS, :] = acc
    mg = mg_ref[...]
    ya = _group_ln(ybuf[...], mg, gcl_ref[...], bcl_ref[...])
    ya = ya * jax.nn.sigmoid(ya)

    z = jax.nn.gelu(p[:, 2 * D_CONV:])
    u = z[:, :D_GMLP]
    v = _group_ln(z[:, D_GMLP:], mg, gvl_ref[...], bvl_ref[...])
    row = lax.broadcasted_iota(jnp.int32, (CHUNK, CHUNK), 0)
    col = lax.broadcasted_iota(jnp.int32, (CHUNK, CHUNK), 1)
    lane_head = lax.broadcasted_iota(jnp.int32, (CHUNK, D_GMLP), 1) // GROUP
    w_heads = [jnp.where(row >= col, ws_ref[hh], 0.0).astype(BF16) for hh in range(N_HEADS_G)]
    yb_chunks = []
    for ch in range(TS // CHUNK):
        vc = v[ch * CHUNK:(ch + 1) * CHUNK, :]
        mixed = bs_ref[...]
        for hh in range(N_HEADS_G):
            vm = jnp.where(lane_head == hh, vc, 0.0).astype(BF16)
            mixed = mixed + jnp.dot(w_heads[hh], vm, preferred_element_type=F32)
        yb_chunks.append(u[ch * CHUNK:(ch + 1) * CHUNK, :] * mixed)
    yb = jnp.concatenate(yb_chunks, axis=0)

    y = (jnp.dot(ya.astype(BF16), wout_ref[0:D_CONV, :], preferred_element_type=F32)
         + jnp.dot(yb.astype(BF16), wout_ref[D_CONV:, :], preferred_element_type=F32)
         + bout_ref[...])
    x1_ref[0] = x + gt1 * y


def _mixer_call(x, cond3, g1, win, bin_, cw, cb, gcl, bcl, gvl, bvl, ws, bs, mg, wout, bout):
    B, S, D = x.shape
    full = lambda shape: pl.BlockSpec(shape, lambda b, s: (0,) * len(shape))
    return pl.pallas_call(
        _mixer_kernel,
        grid=(B, S // TS),
        in_specs=[pl.BlockSpec((1, TS, D), lambda b, s: (b, s, 0)),
                  pl.BlockSpec((1, N_COND, D), lambda b, s: (b, 0, 0)),
                  full((1, D)), full((D, N_PROJ)), full((1, N_PROJ)),
                  full((CONV_WIDTH, D_CONV)), full((1, D_CONV)),
                  full((1, D_CONV)), full((1, D_CONV)), full((1, D_GMLP)), full((1, D_GMLP)),
                  full((N_HEADS_G, CHUNK, CHUNK)), full((CHUNK, D_GMLP)),
                  full((D_CONV, D_CONV)), full((D, D)), full((1, D))],
        out_specs=pl.BlockSpec((1, TS, D), lambda b, s: (b, s, 0)),
        out_shape=jax.ShapeDtypeStruct((B, S, D), F32),
        scratch_shapes=[pltpu.VMEM((TS + CONV_HALO, D_CONV), F32),
                        pltpu.VMEM((TS, D_CONV), F32)],
        compiler_params=pltpu.CompilerParams(
            dimension_semantics=("arbitrary", "arbitrary"), vmem_limit_bytes=VMEM_LIMIT),
        name="mixer",
    )(x, cond3, g1, win, bin_, cw, cb, gcl, bcl, gvl, bvl, ws, bs, mg, wout, bout)


def _oddeven_merge(lo, hi, r):
    step = r * 2
    if step < hi - lo:
        yield from _oddeven_merge(lo, hi, step)
        yield from _oddeven_merge(lo + r, hi, step)
        yield from [(i, i + r) for i in range(lo + r, hi - r, step)]
    else:
        yield (lo, lo + r)


def _oddeven_merge_sort(lo, hi):
    if hi - lo >= 1:
        mid = lo + (hi - lo) // 2
        yield from _oddeven_merge_sort(lo, mid)
        yield from _oddeven_merge_sort(mid + 1, hi)
        yield from _oddeven_merge(lo, hi, 1)


SORT16 = tuple(_oddeven_merge_sort(0, TOPK - 1))
BITONIC16 = tuple((k, k + d) for d in (8, 4, 2, 1) for k in range(TOPK) if not k & d)
SUBLANES = 8


def _exchange(v, pairs):
    for i, j in pairs:
        v[i], v[j] = jnp.maximum(v[i], v[j]), jnp.minimum(v[i], v[j])
    return v


def _top16_sorted(s):
    v = _exchange([s[SUBLANES * k:SUBLANES * (k + 1), :] for k in range(TOPK)], SORT16)
    for shift in (4, 2, 1):
        v = _exchange([jnp.maximum(v[k], pltpu.roll(v[TOPK - 1 - k], shift, axis=0))
                       for k in range(TOPK)], BITONIC16)
    return v


def _top_values(work, n):
    vals = []
    for _ in range(n):
        m = jnp.max(work, axis=0, keepdims=True)
        vals.append(m)
        work = jnp.where(work == m, -jnp.inf, work)
    return vals


def _rows_of(vals, sublane):
    out = vals[SUBLANES - 1]
    for r in range(SUBLANES - 2, -1, -1):
        out = jnp.where(sublane == r, vals[r], out)
    return out


def _prep_kernel(x1_ref, cond_ref, g2_ref, wq_ref, keys_ref,
                 h2t_ref, rk_ref, e1_ref, cnt_ref, e0_ref, q_scr, s_scr):
    cond = cond_ref[0]
    sh2, sc2 = cond[3:4], cond[4:5]
    h2 = _rms(x1_ref[...]) * g2_ref[...] * (1.0 + sc2) + sh2
    h2t_ref[...] = h2.T.astype(BF16)
    q_scr[...] = jnp.dot(h2.astype(BF16), wq_ref[...], preferred_element_type=F32)

    def head(hh, carry):
        c0 = pl.multiple_of(hh * (2 * D_HALF), 2 * D_HALF)
        q0 = q_scr[:, pl.ds(c0, D_HALF)].astype(BF16)
        q1 = q_scr[:, pl.ds(c0 + D_HALF, D_HALF)].astype(BF16)
        nt = (((1,), (1,)), ((), ()))
        s_scr[0] = lax.dot_general(keys_ref[2 * hh], q0, nt, preferred_element_type=F32)
        s_scr[1] = lax.dot_general(keys_ref[2 * hh + 1], q1, nt, preferred_element_type=F32)

        def chunk(lc, carry2):
            lanes = pl.ds(pl.multiple_of(lc * LANES, LANES), LANES)
            s0 = s_scr[0, :, lanes]
            s1 = s_scr[1, :, lanes]
            a = _top16_sorted(s0)
            b = _top16_sorted(s1)
            sublane = lax.broadcasted_iota(jnp.int32, (SUBLANES, LANES), 0)
            a_lo, a_hi = _rows_of(a[:SUBLANES], sublane), _rows_of(a[SUBLANES:], sublane)
            b_hi = _rows_of(b[SUBLANES:], sublane)
            cand = [a_lo + b[0], a_hi + b[0]] + [a_lo + b[c] for c in range(1, SUBLANES)]
            cand.append(a[0] + b_hi)
            top = _top_values(jnp.concatenate(cand, axis=0), TOPK)
            tau = top[TOPK - 1]
            zsum = jnp.zeros_like(tau)
            for t in top:
                zsum = zsum + jnp.exp(t - top[0])
            rank1 = jnp.zeros(s1.shape, F32)
            count = jnp.zeros(s0.shape, F32)
            for c in range(TOPK):
                bc = jnp.concatenate([b[c]] * (N_KEYS // SUBLANES), axis=0)
                rank1 = jnp.where(bc > s1, float(c + 1), rank1)
                count = jnp.where(s0 + bc >= tau, float(c + 1), count)
            a0 = jnp.concatenate([a[0]] * (N_KEYS // SUBLANES), axis=0)
            b0 = jnp.concatenate([b[0]] * (N_KEYS // SUBLANES), axis=0)
            rk_ref[hh, :, lanes] = rank1.astype(BF16)
            e1_ref[hh, :, lanes] = (jnp.exp(s1 - b0) / zsum).astype(BF16)
            cnt_ref[hh, :, lanes] = count
            e0_ref[hh, :, lanes] = jnp.exp(s0 - a0)
            return carry2

        lax.fori_loop(0, TB // LANES, chunk, 0)
        return carry

    lax.fori_loop(0, PEER_HEADS, head, 0)


def _prep_call(x1, cond3, g2, wq, keys, tiles_per_batch):
    T, D = x1.shape
    tok = lambda: pl.BlockSpec((PEER_HEADS, N_KEYS, TB), lambda i: (0, 0, i))
    sds = lambda dt: jax.ShapeDtypeStruct((PEER_HEADS, N_KEYS, T), dt)
    return pl.pallas_call(
        _prep_kernel,
        grid=(T // TB,),
        in_specs=[pl.BlockSpec((TB, D), lambda i: (i, 0)),
                  pl.BlockSpec((1, N_COND, D), lambda i: (i // tiles_per_batch, 0, 0)),
                  pl.BlockSpec((1, D), lambda i: (0, 0)),
                  pl.BlockSpec((D, 2 * PEER_HEADS * D_HALF), lambda i: (0, 0)),
                  pl.BlockSpec((2 * PEER_HEADS, N_KEYS, D_HALF), lambda i: (0, 0, 0))],
        out_specs=[pl.BlockSpec((D, TB), lambda i: (0, i)), tok(), tok(), tok(), tok()],
        out_shape=[jax.ShapeDtypeStruct((D, T), BF16),
                   sds(BF16), sds(BF16), sds(F32), sds(F32)],
        scratch_shapes=[pltpu.VMEM((TB, 2 * PEER_HEADS * D_HALF), F32),
                        pltpu.VMEM((2, N_KEYS, TB), F32)],
        compiler_params=pltpu.CompilerParams(
            dimension_semantics=("arbitrary",), vmem_limit_bytes=VMEM_LIMIT),
        name="prep",
    )(x1, cond3, g2, wq, keys)


def _peer_kernel(h2t_ref, rk_ref, e1_ref, cnt_ref, e0_ref, u_ref, vt_ref,
                 x1_ref, cond_ref, gf_ref, o_ref, rk_scr, e1_scr, a_scr, z_scr, acc_scr):
    j = pl.program_id(1)

    @pl.when(j == 0)
    def _():
        acc_scr[...] = jnp.zeros_like(acc_scr)
        rk_scr[...] = rk_ref[...]
        e1_scr[...] = e1_ref[...]

    zero = jnp.zeros((BF16_ROWS, LANES), BF16)
    jtiles = N_KEYS // BF16_ROWS
    for st in range(EB // STAGE):
        erows = slice(st * STAGE, (st + 1) * STAGE)
        a_scr[st] = jnp.dot(u_ref[erows, :], h2t_ref[...], preferred_element_type=F32)
        for ib in range(STAGE // N_KEYS // GATE_IB):
            for lc in range(TB // LANES):
                lanes = slice(lc * LANES, (lc + 1) * LANES)
                g = [[None] * jtiles for _ in range(GATE_IB)]
                for hh in range(PEER_HEADS):
                    rk = [rk_scr[hh, q * BF16_ROWS:(q + 1) * BF16_ROWS, lanes]
                          for q in range(jtiles)]
                    e1 = [e1_scr[hh, q * BF16_ROWS:(q + 1) * BF16_ROWS, lanes]
                          for q in range(jtiles)]
                    for p in range(GATE_IB):
                        ii = st * (STAGE // N_KEYS) + ib * GATE_IB + p
                        cnt = jnp.broadcast_to(cnt_ref[hh, ii:ii + 1, lanes],
                                               (BF16_ROWS, LANES)).astype(BF16)
                        e0 = jnp.broadcast_to(e0_ref[hh, ii:ii + 1, lanes],
                                              (BF16_ROWS, LANES)).astype(BF16)
                        for q in range(jtiles):
                            term = jnp.where(rk[q] < cnt, e1[q], zero) * e0
                            g[p][q] = term if g[p][q] is None else g[p][q] + term
                for p in range(GATE_IB):
                    for q in range(jtiles):
                        r0 = (ib * GATE_IB + p) * N_KEYS + q * BF16_ROWS
                        act = jax.nn.gelu(a_scr[st, r0:r0 + BF16_ROWS, lanes].astype(BF16))
                        z_scr[st, r0:r0 + BF16_ROWS, lanes] = act * g[p][q]
        acc_scr[...] += jnp.dot(vt_ref[0, :, erows], z_scr[st], preferred_element_type=F32)

    @pl.when(j == pl.num_programs(1) - 1)
    def _():
        gt2 = cond_ref[0][5:6]
        x2 = x1_ref[...] + gt2 * acc_scr[...].T
        o_ref[...] = _rms(x2) * gf_ref[...]


def _peer_call(h2t, rk, e1, cnt, e0, u_bf, vt_bf, x1, cond3, gf, tiles_per_batch):
    T, D = x1.shape
    tok = lambda: pl.BlockSpec((PEER_HEADS, N_KEYS, TB), lambda i, j: (0, 0, i))
    rows = lambda: pl.BlockSpec((PEER_HEADS, EB // N_KEYS, TB), lambda i, j: (0, j, i))
    return pl.pallas_call(
        _peer_kernel,
        grid=(T // TB, N_EXPERTS // EB),
        in_specs=[pl.BlockSpec((D, TB), lambda i, j: (0, i)),
                  tok(), tok(), rows(), rows(),
                  pl.BlockSpec((EB, D), lambda i, j: (j, 0)),
                  pl.BlockSpec((1, D, EB), lambda i, j: (j, 0, 0)),
                  pl.BlockSpec((TB, D), lambda i, j: (i, 0)),
                  pl.BlockSpec((1, N_COND, D), lambda i, j: (i // tiles_per_batch, 0, 0)),
                  pl.BlockSpec((1, D), lambda i, j: (0, 0))],
        out_specs=pl.BlockSpec((TB, D), lambda i, j: (i, 0)),
        out_shape=jax.ShapeDtypeStruct((T, D), F32),
        scratch_shapes=[pltpu.VMEM((PEER_HEADS, N_KEYS, TB), BF16),
                        pltpu.VMEM((PEER_HEADS, N_KEYS, TB), BF16),
                        pltpu.VMEM((EB // STAGE, STAGE, TB), F32),
                        pltpu.VMEM((EB // STAGE, STAGE, TB), BF16),
                        pltpu.VMEM((D, TB), F32)],
        compiler_params=pltpu.CompilerParams(
            dimension_semantics=("arbitrary", "arbitrary"), vmem_limit_bytes=VMEM_LIMIT),
        name="peer",
    )(h2t, rk, e1, cnt, e0, u_bf, vt_bf, x1, cond3, gf)


def kernel(x, c, w_ada, b_ada, g_norm1, w_in, b_in, conv_w, conv_b, g_conv_ln, b_conv_ln,
           g_v_ln, b_v_ln, w_spatial, b_spatial, w_out, b_out, g_norm2, w_query, sub_keys,
           expert_u, expert_v, g_final):
    B, S, D = x.shape
    depth = w_ada.shape[0]
    assert depth == 1, "the final rmsnorm is fused into the PEER call of the only layer"
    T = B * S
    row = lambda v: v.reshape(1, -1)
    grp = jnp.arange(D_CONV) // GROUP
    mg = jnp.where(grp[:, None] == grp[None, :], 1.0 / GROUP, 0.0).astype(BF16)
    c_pad = jnp.pad(c, ((0, 8 - B), (0, 0)))
    for l in range(depth):
        cond = _cond_call(c_pad, w_ada[l], row(b_ada[l]))
        cond3 = cond[:B].reshape(B, N_COND, D)
        bs = jnp.repeat(b_spatial[l].T, GROUP, axis=1)
        x1 = _mixer_call(x, cond3, row(g_norm1[l]), w_in[l].astype(BF16), row(b_in[l]),
                         conv_w[l], row(conv_b[l]), row(g_conv_ln[l]), row(b_conv_ln[l]),
                         row(g_v_ln[l]), row(b_v_ln[l]), w_spatial[l], bs, mg,
                         w_out[l].astype(BF16), row(b_out[l]))
        x1 = x1.reshape(T, D)
        keys = sub_keys[l].reshape(2 * PEER_HEADS, N_KEYS, D_HALF).astype(BF16)
        h2t, rk, e1, cnt, e0 = _prep_call(x1, cond3, row(g_norm2[l]), w_query[l].astype(BF16),
                                          keys, S // TB)
        vt = jnp.swapaxes(expert_v[l].astype(BF16).reshape(N_EXPERTS // EB, EB, D), 1, 2)
        x = _peer_call(h2t, rk, e1, cnt, e0, expert_u[l].astype(BF16), vt, x1, cond3,
                       row(g_final), S // TB).reshape(B, S, D)
    return x
```

```python
import functools

import jax
import jax.numpy as jnp
from jax import lax
from jax.experimental import pallas as pl
from jax.experimental.pallas import tpu as pltpu

F32 = jnp.float32
BF16 = jnp.bfloat16

D_MODEL = 1024
D_CONV = 512
D_GMLP = 512
GROUP = 64
N_HEADS_G = 8
CONV_WIDTH = 31
CHUNK = 128
N_PROJ = 2048
N_KEYS = 128
N_EXPERTS = N_KEYS * N_KEYS
PEER_HEADS = 8
D_HALF = 128
TOPK = 16
N_COND = 6
EPS = 1e-6

V7X_VMEM_BYTES = 64 * 1024 * 1024
VMEM_LIMIT = V7X_VMEM_BYTES * 3 // 4

TS = 512
CONV_HALO = 32
CONV_ROWS = 64
TB = 512
EB = 1024
STAGE = 1024
LANES = 128
BF16_ROWS = 16
GATE_QB = 8
GATE_IB = 1


def _rms(x):
    return x * lax.rsqrt(jnp.mean(x * x, axis=-1, keepdims=True) + EPS)


def _split_dot(x, m):
    hi = x.astype(BF16)
    lo = (x - hi.astype(F32)).astype(BF16)
    return (jnp.dot(hi, m, preferred_element_type=F32)
            + jnp.dot(lo, m, preferred_element_type=F32))


def _group_ln(y, mg, g, b):
    mu = _split_dot(y, mg)
    d = y - mu
    var = _split_dot(d * d, mg)
    return d * lax.rsqrt(var + EPS) * g + b


def _cond_kernel(c_ref, w_ref, b_ref, o_ref):
    c = c_ref[...]
    s = c * jax.nn.sigmoid(c)
    o_ref[...] = jnp.dot(s, w_ref[...], preferred_element_type=F32,
                         precision=lax.Precision.HIGHEST) + b_ref[...]


def _cond_call(c_pad, w_ada, b_ada):
    n = w_ada.shape[1]
    blk = 1024
    return pl.pallas_call(
        _cond_kernel,
        grid=(n // blk,),
        in_specs=[pl.BlockSpec((8, D_MODEL), lambda j: (0, 0)),
                  pl.BlockSpec((D_MODEL, blk), lambda j: (0, j)),
                  pl.BlockSpec((1, blk), lambda j: (0, j))],
        out_specs=pl.BlockSpec((8, blk), lambda j: (0, j)),
        out_shape=jax.ShapeDtypeStruct((8, n), F32),
        compiler_params=pltpu.CompilerParams(
            dimension_semantics=("arbitrary",), vmem_limit_bytes=VMEM_LIMIT),
        name="cond",
    )(c_pad, w_ada, b_ada)


def _mixer_kernel(x_ref, cond_ref, g1_ref, win_ref, bin_ref, cw_ref, cb_ref,
                  gcl_ref, bcl_ref, gvl_ref, bvl_ref, ws_ref, bs_ref, mg_ref,
                  wout_ref, bout_ref, x1_ref, abuf, ybuf):
    s = pl.program_id(1)
    x = x_ref[0]
    cond = cond_ref[0]
    sh1, sc1, gt1 = cond[0:1], cond[1:2], cond[2:3]
    h = _rms(x) * g1_ref[...] * (1.0 + sc1) + sh1
    p = jnp.dot(h.astype(BF16), win_ref[...], preferred_element_type=F32) + bin_ref[...]

    a = p[:, :D_CONV] * jax.nn.sigmoid(p[:, D_CONV:2 * D_CONV])

    @pl.when(s == 0)
    def _():
        abuf[0:CONV_HALO, :] = jnp.zeros((CONV_HALO, D_CONV), F32)

    @pl.when(s > 0)
    def _():
        abuf[0:CONV_HALO, :] = abuf[TS:TS + CONV_HALO, :]

    abuf[CONV_HALO:CONV_HALO + TS, :] = a

    off = CONV_HALO - (CONV_WIDTH - 1)

    for r in range(TS // CONV_ROWS):
        base = r * CONV_ROWS
        acc = jnp.zeros((CONV_ROWS, D_CONV), F32) + cb_ref[...]
        for k in range(CONV_WIDTH):
            acc = acc + abuf[base + off + k:base + off + k + CONV_ROWS, :] * cw_ref[k:k + 1, :]
        ybuf[base:base + CONV_ROWS, :] = acc
    mg = mg_ref[...]
    ya = _group_ln(ybuf[...], mg, gcl_ref[...], bcl_ref[...])
    ya = ya * jax.nn.sigmoid(ya)

    z = jax.nn.gelu(p[:, 2 * D_CONV:])
    u = z[:, :D_GMLP]
    v = _group_ln(z[:, D_GMLP:], mg, gvl_ref[...], bvl_ref[...])
    row = lax.broadcasted_iota(jnp.int32, (CHUNK, CHUNK), 0)
    col = lax.broadcasted_iota(jnp.int32, (CHUNK, CHUNK), 1)
    lane_head = lax.broadcasted_iota(jnp.int32, (CHUNK, D_GMLP), 1) // GROUP
    w_heads = [jnp.where(row >= col, ws_ref[hh], 0.0).astype(BF16) for hh in range(N_HEADS_G)]
    yb_chunks = []
    for ch in range(TS // CHUNK):
        vc = v[ch * CHUNK:(ch + 1) * CHUNK, :]
        mixed = bs_ref[...]
        for hh in range(N_HEADS_G):
            vm = jnp.where(lane_head == hh, vc, 0.0).astype(BF16)
            mixed = mixed + jnp.dot(w_heads[hh], vm, preferred_element_type=F32)
        yb_chunks.append(u[ch * CHUNK:(ch + 1) * CHUNK, :] * mixed)
    yb = jnp.concatenate(yb_chunks, axis=0)

    y = (jnp.dot(ya.astype(BF16), wout_ref[0:D_CONV, :], preferred_element_type=F32)
         + jnp.dot(yb.astype(BF16), wout_ref[D_CONV:, :], preferred_element_type=F32)
         + bout_ref[...])
    x1_ref[0] = x + gt1 * y


def _mixer_call(x, cond3, g1, win, bin_, cw, cb, gcl, bcl, gvl, bvl, ws, bs, mg, wout, bout):
    B, S, D = x.shape
    full = lambda shape: pl.BlockSpec(shape, lambda b, s: (0,) * len(shape))
    return pl.pallas_call(
        _mixer_kernel,
        grid=(B, S // TS),
        in_specs=[pl.BlockSpec((1, TS, D), lambda b, s: (b, s, 0)),
                  pl.BlockSpec((1, N_COND, D), lambda b, s: (b, 0, 0)),
                  full((1, D)), full((D, N_PROJ)), full((1, N_PROJ)),
                  full((CONV_WIDTH, D_CONV)), full((1, D_CONV)),
                  full((1, D_CONV)), full((1, D_CONV)), full((1, D_GMLP)), full((1, D_GMLP)),
                  full((N_HEADS_G, CHUNK, CHUNK)), full((CHUNK, D_GMLP)),
                  full((D_CONV, D_CONV)), full((D, D)), full((1, D))],
        out_specs=pl.BlockSpec((1, TS, D), lambda b, s: (b, s, 0)),
        out_shape=jax.ShapeDtypeStruct((B, S, D), F32),
        scratch_shapes=[pltpu.VMEM((TS + CONV_HALO, D_CONV), F32),
                        pltpu.VMEM((TS, D_CONV), F32)],
        compiler_params=pltpu.CompilerParams(
            dimension_semantics=("arbitrary", "arbitrary"), vmem_limit_bytes=VMEM_LIMIT),
        name="mixer",
    )(x, cond3, g1, win, bin_, cw, cb, gcl, bcl, gvl, bvl, ws, bs, mg, wout, bout)


def _oddeven_merge(lo, hi, r):
    step = r * 2
    if step < hi - lo:
        yield from _oddeven_merge(lo, hi, step)
        yield from _oddeven_merge(lo + r, hi, step)
        yield from [(i, i + r) for i in range(lo + r, hi - r, step)]
    else:
        yield (lo, lo + r)


def _oddeven_merge_sort(lo, hi):
    if hi - lo >= 1:
        mid = lo + (hi - lo) // 2
        yield from _oddeven_merge_sort(lo, mid)
        yield from _oddeven_merge_sort(mid + 1, hi)
        yield from _oddeven_merge(lo, hi, 1)


SORT16 = tuple(_oddeven_merge_sort(0, TOPK - 1))
BITONIC16 = tuple((k, k + d) for d in (8, 4, 2, 1) for k in range(TOPK) if not k & d)
SUBLANES = 8


def _exchange(v, pairs):
    for i, j in pairs:
        v[i], v[j] = jnp.maximum(v[i], v[j]), jnp.minimum(v[i], v[j])
    return v


def _top16_sorted(s):
    v = _exchange([s[SUBLANES * k:SUBLANES * (k + 1), :] for k in range(TOPK)], SORT16)
    for shift in (4, 2, 1):
        v = _exchange([jnp.maximum(v[k], pltpu.roll(v[TOPK - 1 - k], shift, axis=0))
                       for k in range(TOPK)], BITONIC16)
    return v


def _top_values(work, n):
    vals = []
    for _ in range(n):
        m = jnp.max(work, axis=0, keepdims=True)
        vals.append(m)
        work = jnp.where(work == m, -jnp.inf, work)
    return vals


def _rows_of(vals, sublane):
    out = vals[SUBLANES - 1]
    for r in range(SUBLANES - 2, -1, -1):
        out = jnp.where(sublane == r, vals[r], out)
    return out


def _prep_kernel(x1_ref, cond_ref, g2_ref, wq_ref, keys_ref,
                 h2t_ref, rk_ref, e1_ref, cnt_ref, e0_ref, q_scr, s_scr):
    cond = cond_ref[0]
    sh2, sc2 = cond[3:4], cond[4:5]
    h2 = _rms(x1_ref[...]) * g2_ref[...] * (1.0 + sc2) + sh2
    h2t_ref[...] = h2.T.astype(BF16)
    q_scr[...] = jnp.dot(h2.astype(BF16), wq_ref[...], preferred_element_type=F32)

    def head(hh, carry):
        c0 = pl.multiple_of(hh * (2 * D_HALF), 2 * D_HALF)
        q0 = q_scr[:, pl.ds(c0, D_HALF)].astype(BF16)
        q1 = q_scr[:, pl.ds(c0 + D_HALF, D_HALF)].astype(BF16)
        nt = (((1,), (1,)), ((), ()))
        s_scr[0] = lax.dot_general(keys_ref[2 * hh], q0, nt, preferred_element_type=F32)
        s_scr[1] = lax.dot_general(keys_ref[2 * hh + 1], q1, nt, preferred_element_type=F32)

        def chunk(lc, carry2):
            lanes = pl.ds(pl.multiple_of(lc * LANES, LANES), LANES)
            s0 = s_scr[0, :, lanes]
            s1 = s_scr[1, :, lanes]
            a = _top16_sorted(s0)
            b = _top16_sorted(s1)
            sublane = lax.broadcasted_iota(jnp.int32, (SUBLANES, LANES), 0)
            a_lo, a_hi = _rows_of(a[:SUBLANES], sublane), _rows_of(a[SUBLANES:], sublane)
            b_hi = _rows_of(b[SUBLANES:], sublane)
            cand = [a_lo + b[0], a_hi + b[0]] + [a_lo + b[c] for c in range(1, SUBLANES)]
            cand.append(a[0] + b_hi)
            top = _top_values(jnp.concatenate(cand, axis=0), TOPK)
            tau = top[TOPK - 1]
            zsum = jnp.zeros_like(tau)
            for t in top:
                zsum = zsum + jnp.exp(t - top[0])
            rank1 = jnp.zeros(s1.shape, F32)
            count = jnp.zeros(s0.shape, F32)
            for c in range(TOPK):
                bc = jnp.concatenate([b[c]] * (N_KEYS // SUBLANES), axis=0)
                rank1 = jnp.where(bc > s1, float(c + 1), rank1)
                count = jnp.where(s0 + bc >= tau, float(c + 1), count)
            a0 = jnp.concatenate([a[0]] * (N_KEYS // SUBLANES), axis=0)
            b0 = jnp.concatenate([b[0]] * (N_KEYS // SUBLANES), axis=0)
            rk_ref[hh, :, lanes] = rank1.astype(BF16)
            e1_ref[hh, :, lanes] = (jnp.exp(s1 - b0) / zsum).astype(BF16)
            cnt_ref[hh, :, lanes] = count
            e0_ref[hh, :, lanes] = jnp.exp(s0 - a0)
            return carry2

        lax.fori_loop(0, TB // LANES, chunk, 0)
        return carry

    lax.fori_loop(0, PEER_HEADS, head, 0)


def _prep_call(x1, cond3, g2, wq, keys, tiles_per_batch):
    T, D = x1.shape
    tok = lambda: pl.BlockSpec((PEER_HEADS, N_KEYS, TB), lambda i: (0, 0, i))
    sds = lambda dt: jax.ShapeDtypeStruct((PEER_HEADS, N_KEYS, T), dt)
    return pl.pallas_call(
        _prep_kernel,
        grid=(T // TB,),
        in_specs=[pl.BlockSpec((TB, D), lambda i: (i, 0)),
                  pl.BlockSpec((1, N_COND, D), lambda i: (i // tiles_per_batch, 0, 0)),
                  pl.BlockSpec((1, D), lambda i: (0, 0)),
                  pl.BlockSpec((D, 2 * PEER_HEADS * D_HALF), lambda i: (0, 0)),
                  pl.BlockSpec((2 * PEER_HEADS, N_KEYS, D_HALF), lambda i: (0, 0, 0))],
        out_specs=[pl.BlockSpec((D, TB), lambda i: (0, i)), tok(), tok(), tok(), tok()],
        out_shape=[jax.ShapeDtypeStruct((D, T), BF16),
                   sds(BF16), sds(BF16), sds(F32), sds(F32)],
        scratch_shapes=[pltpu.VMEM((TB, 2 * PEER_HEADS * D_HALF), F32),
                        pltpu.VMEM((2, N_KEYS, TB), F32)],
        compiler_params=pltpu.CompilerParams(
            dimension_semantics=("arbitrary",), vmem_limit_bytes=VMEM_LIMIT),
        name="prep",
    )(x1, cond3, g2, wq, keys)


def _peer_kernel(h2t_ref, rk_ref, e1_ref, cnt_ref, e0_ref, u_ref, vt_ref,
                 x1_ref, cond_ref, gf_ref, o_ref, rk_scr, e1_scr, a_scr, z_scr, acc_scr):
    j = pl.program_id(1)

    @pl.when(j == 0)
    def _():
        acc_scr[...] = jnp.zeros_like(acc_scr)
        for lc in range(TB // LANES):
            rk_scr[:, lc] = rk_ref[:, :, lc * LANES:(lc + 1) * LANES]
            e1_scr[:, lc] = e1_ref[:, :, lc * LANES:(lc + 1) * LANES]

    zero = jnp.zeros((BF16_ROWS, LANES), BF16)
    jtiles = N_KEYS // BF16_ROWS
    for st in range(EB // STAGE):
        erows = slice(st * STAGE, (st + 1) * STAGE)
        a_scr[st] = jnp.dot(u_ref[erows, :], h2t_ref[...], preferred_element_type=F32)
        for ib in range(STAGE // N_KEYS // GATE_IB):
            for lc in range(TB // LANES):
                lanes = slice(lc * LANES, (lc + 1) * LANES)
                for qb in range(jtiles // GATE_QB):
                    qs = range(qb * GATE_QB, (qb + 1) * GATE_QB)
                    g = [[None] * GATE_QB for _ in range(GATE_IB)]
                    for hh in range(PEER_HEADS):
                        rk = [rk_scr[hh, lc, q * BF16_ROWS:(q + 1) * BF16_ROWS, :] for q in qs]
                        e1 = [e1_scr[hh, lc, q * BF16_ROWS:(q + 1) * BF16_ROWS, :] for q in qs]
                        for p in range(GATE_IB):
                            ii = st * (STAGE // N_KEYS) + ib * GATE_IB + p
                            cnt = jnp.broadcast_to(cnt_ref[hh, ii:ii + 1, lanes],
                                                   (BF16_ROWS, LANES)).astype(BF16)
                            e0 = jnp.broadcast_to(e0_ref[hh, ii:ii + 1, lanes],
                                                  (BF16_ROWS, LANES)).astype(BF16)
                            for k in range(GATE_QB):
                                term = jnp.minimum(jnp.maximum(cnt - rk[k], zero), e0) * e1[k]
                                g[p][k] = term if g[p][k] is None else g[p][k] + term
                    for p in range(GATE_IB):
                        for k, q in enumerate(qs):
                            r0 = (ib * GATE_IB + p) * N_KEYS + q * BF16_ROWS
                            act = jax.nn.gelu(a_scr[st, r0:r0 + BF16_ROWS, lanes].astype(BF16))
                            z_scr[st, r0:r0 + BF16_ROWS, lanes] = act * g[p][k]
        acc_scr[...] += jnp.dot(vt_ref[0, :, erows], z_scr[st], preferred_element_type=F32)

    @pl.when(j == pl.num_programs(1) - 1)
    def _():
        gt2 = cond_ref[0][5:6]
        x2 = x1_ref[...] + gt2 * acc_scr[...].T
        o_ref[...] = _rms(x2) * gf_ref[...]


def _peer_call(h2t, rk, e1, cnt, e0, u_bf, vt_bf, x1, cond3, gf, tiles_per_batch):
    T, D = x1.shape
    tok = lambda: pl.BlockSpec((PEER_HEADS, N_KEYS, TB), lambda i, j: (0, 0, i))
    rows = lambda: pl.BlockSpec((PEER_HEADS, EB // N_KEYS, TB), lambda i, j: (0, j, i))
    return pl.pallas_call(
        _peer_kernel,
        grid=(T // TB, N_EXPERTS // EB),
        in_specs=[pl.BlockSpec((D, TB), lambda i, j: (0, i)),
                  tok(), tok(), rows(), rows(),
                  pl.BlockSpec((EB, D), lambda i, j: (j, 0)),
                  pl.BlockSpec((1, D, EB), lambda i, j: (j, 0, 0)),
                  pl.BlockSpec((TB, D), lambda i, j: (i, 0)),
                  pl.BlockSpec((1, N_COND, D), lambda i, j: (i // tiles_per_batch, 0, 0)),
                  pl.BlockSpec((1, D), lambda i, j: (0, 0))],
        out_specs=pl.BlockSpec((TB, D), lambda i, j: (i, 0)),
        out_shape=jax.ShapeDtypeStruct((T, D), F32),
        scratch_shapes=[pltpu.VMEM((PEER_HEADS, TB // LANES, N_KEYS, LANES), BF16),
                        pltpu.VMEM((PEER_HEADS, TB // LANES, N_KEYS, LANES), BF16),
                        pltpu.VMEM((EB // STAGE, STAGE, TB), F32),
                        pltpu.VMEM((EB // STAGE, STAGE, TB), BF16),
                        pltpu.VMEM((D, TB), F32)],
        compiler_params=pltpu.CompilerParams(
            dimension_semantics=("arbitrary", "arbitrary"), vmem_limit_bytes=VMEM_LIMIT),
        name="peer",
    )(h2t, rk, e1, cnt, e0, u_bf, vt_bf, x1, cond3, gf)


def kernel(x, c, w_ada, b_ada, g_norm1, w_in, b_in, conv_w, conv_b, g_conv_ln, b_conv_ln,
           g_v_ln, b_v_ln, w_spatial, b_spatial, w_out, b_out, g_norm2, w_query, sub_keys,
           expert_u, expert_v, g_final):
    B, S, D = x.shape
    depth = w_ada.shape[0]
    assert depth == 1, "the final rmsnorm is fused into the PEER call of the only layer"
    T = B * S
    row = lambda v: v.reshape(1, -1)
    grp = jnp.arange(D_CONV) // GROUP
    mg = jnp.where(grp[:, None] == grp[None, :], 1.0 / GROUP, 0.0).astype(BF16)
    c_pad = jnp.pad(c, ((0, 8 - B), (0, 0)))
    for l in range(depth):
        cond = _cond_call(c_pad, w_ada[l], row(b_ada[l]))
        cond3 = cond[:B].reshape(B, N_COND, D)
        bs = jnp.repeat(b_spatial[l].T, GROUP, axis=1)
        x1 = _mixer_call(x, cond3, row(g_norm1[l]), w_in[l].astype(BF16), row(b_in[l]),
                         conv_w[l], row(conv_b[l]), row(g_conv_ln[l]), row(b_conv_ln[l]),
                         row(g_v_ln[l]), row(b_v_ln[l]), w_spatial[l], bs, mg,
                         w_out[l].astype(BF16), row(b_out[l]))
        x1 = x1.reshape(T, D)
        keys = sub_keys[l].reshape(2 * PEER_HEADS, N_KEYS, D_HALF).astype(BF16)
        h2t, rk, e1, cnt, e0 = _prep_call(x1, cond3, row(g_norm2[l]), w_query[l].astype(BF16),
                                          keys, S // TB)
        vt = jnp.swapaxes(expert_v[l].astype(BF16).reshape(N_EXPERTS // EB, EB, D), 1, 2)
        x = _peer_call(h2t, rk, e1, cnt, e0, expert_u[l].astype(BF16), vt, x1, cond3,
                       row(g_final), S // TB).reshape(B, S, D)
    return x
```

```python
import functools

import jax
import jax.numpy as jnp
from jax import lax
from jax.experimental import pallas as pl
from jax.experimental.pallas import tpu as pltpu

F32 = jnp.float32
BF16 = jnp.bfloat16

D_MODEL = 1024
D_CONV = 512
D_GMLP = 512
GROUP = 64
N_HEADS_G = 8
CONV_WIDTH = 31
CHUNK = 128
N_PROJ = 2048
N_KEYS = 128
N_EXPERTS = N_KEYS * N_KEYS
PEER_HEADS = 8
D_HALF = 128
TOPK = 16
N_COND = 6
EPS = 1e-6

V7X_VMEM_BYTES = 64 * 1024 * 1024
VMEM_LIMIT = V7X_VMEM_BYTES * 3 // 4

TS = 512
CONV_HALO = 32
CONV_ROWS = 64
TB = 512
EB = 1024
WINDOW_ROWS = 2
LANES = 128
MXU_K = 256
BF16_ROWS = 16


def _rms(x):
    return x * lax.rsqrt(jnp.mean(x * x, axis=-1, keepdims=True) + EPS)


def _split_dot(x, m):
    hi = x.astype(BF16)
    lo = (x - hi.astype(F32)).astype(BF16)
    return (jnp.dot(hi, m, preferred_element_type=F32)
            + jnp.dot(lo, m, preferred_element_type=F32))


def _group_ln(y, mg, g, b):
    mu = _split_dot(y, mg)
    d = y - mu
    var = _split_dot(d * d, mg)
    return d * lax.rsqrt(var + EPS) * g + b


def _cond_kernel(c_ref, w_ref, b_ref, o_ref):
    c = c_ref[...]
    s = c * jax.nn.sigmoid(c)
    o_ref[...] = jnp.dot(s, w_ref[...], preferred_element_type=F32,
                         precision=lax.Precision.HIGHEST) + b_ref[...]


def _cond_call(c_pad, w_ada, b_ada):
    n = w_ada.shape[1]
    blk = 1024
    return pl.pallas_call(
        _cond_kernel,
        grid=(n // blk,),
        in_specs=[pl.BlockSpec((8, D_MODEL), lambda j: (0, 0)),
                  pl.BlockSpec((D_MODEL, blk), lambda j: (0, j)),
                  pl.BlockSpec((1, blk), lambda j: (0, j))],
        out_specs=pl.BlockSpec((8, blk), lambda j: (0, j)),
        out_shape=jax.ShapeDtypeStruct((8, n), F32),
        compiler_params=pltpu.CompilerParams(
            dimension_semantics=("arbitrary",), vmem_limit_bytes=VMEM_LIMIT),
        name="cond",
    )(c_pad, w_ada, b_ada)


def _mixer_kernel(x_ref, cond_ref, g1_ref, win_ref, bin_ref, cw_ref, cb_ref,
                  gcl_ref, bcl_ref, gvl_ref, bvl_ref, ws_ref, bs_ref, mg_ref,
                  wout_ref, bout_ref, x1_ref, abuf, ybuf):
    s = pl.program_id(1)
    x = x_ref[0]
    cond = cond_ref[0]
    sh1, sc1, gt1 = cond[0:1], cond[1:2], cond[2:3]
    h = _rms(x) * g1_ref[...] * (1.0 + sc1) + sh1
    p = jnp.dot(h.astype(BF16), win_ref[...], preferred_element_type=F32) + bin_ref[...]

    a = p[:, :D_CONV] * jax.nn.sigmoid(p[:, D_CONV:2 * D_CONV])

    @pl.when(s == 0)
    def _():
        abuf[0:CONV_HALO, :] = jnp.zeros((CONV_HALO, D_CONV), F32)

    @pl.when(s > 0)
    def _():
        abuf[0:CONV_HALO, :] = abuf[TS:TS + CONV_HALO, :]

    abuf[CONV_HALO:CONV_HALO + TS, :] = a

    off = CONV_HALO - (CONV_WIDTH - 1)

    for r in range(TS // CONV_ROWS):
        base = r * CONV_ROWS
        acc = jnp.zeros((CONV_ROWS, D_CONV), F32) + cb_ref[...]
        for k in range(CONV_WIDTH):
            acc = acc + abuf[base + off + k:base + off + k + CONV_ROWS, :] * cw_ref[k:k + 1, :]
        ybuf[base:base + CONV_ROWS, :] = acc
    mg = mg_ref[...]
    ya = _group_ln(ybuf[...], mg, gcl_ref[...], bcl_ref[...])
    ya = ya * jax.nn.sigmoid(ya)

    z = jax.nn.gelu(p[:, 2 * D_CONV:])
    u = z[:, :D_GMLP]
    v = _group_ln(z[:, D_GMLP:], mg, gvl_ref[...], bvl_ref[...])
    row = lax.broadcasted_iota(jnp.int32, (CHUNK, CHUNK), 0)
    col = lax.broadcasted_iota(jnp.int32, (CHUNK, CHUNK), 1)
    lane_head = lax.broadcasted_iota(jnp.int32, (CHUNK, D_GMLP), 1) // GROUP
    w_heads = [jnp.where(row >= col, ws_ref[hh], 0.0).astype(BF16) for hh in range(N_HEADS_G)]
    yb_chunks = []
    for ch in range(TS // CHUNK):
        vc = v[ch * CHUNK:(ch + 1) * CHUNK, :]
        mixed = bs_ref[...]
        for hh in range(N_HEADS_G):
            vm = jnp.where(lane_head == hh, vc, 0.0).astype(BF16)
            mixed = mixed + jnp.dot(w_heads[hh], vm, preferred_element_type=F32)
        yb_chunks.append(u[ch * CHUNK:(ch + 1) * CHUNK, :] * mixed)
    yb = jnp.concatenate(yb_chunks, axis=0)

    y = (jnp.dot(ya.astype(BF16), wout_ref[0:D_CONV, :], preferred_element_type=F32)
         + jnp.dot(yb.astype(BF16), wout_ref[D_CONV:, :], preferred_element_type=F32)
         + bout_ref[...])
    x1_ref[0] = x + gt1 * y


def _mixer_call(x, cond3, g1, win, bin_, cw, cb, gcl, bcl, gvl, bvl, ws, bs, mg, wout, bout):
    B, S, D = x.shape
    full = lambda shape: pl.BlockSpec(shape, lambda b, s: (0,) * len(shape))
    return pl.pallas_call(
        _mixer_kernel,
        grid=(B, S // TS),
        in_specs=[pl.BlockSpec((1, TS, D), lambda b, s: (b, s, 0)),
                  pl.BlockSpec((1, N_COND, D), lambda b, s: (b, 0, 0)),
                  full((1, D)), full((D, N_PROJ)), full((1, N_PROJ)),
                  full((CONV_WIDTH, D_CONV)), full((1, D_CONV)),
                  full((1, D_CONV)), full((1, D_CONV)), full((1, D_GMLP)), full((1, D_GMLP)),
                  full((N_HEADS_G, CHUNK, CHUNK)), full((CHUNK, D_GMLP)),
                  full((D_CONV, D_CONV)), full((D, D)), full((1, D))],
        out_specs=pl.BlockSpec((1, TS, D), lambda b, s: (b, s, 0)),
        out_shape=jax.ShapeDtypeStruct((B, S, D), F32),
        scratch_shapes=[pltpu.VMEM((TS + CONV_HALO, D_CONV), F32),
                        pltpu.VMEM((TS, D_CONV), F32)],
        compiler_params=pltpu.CompilerParams(
            dimension_semantics=("arbitrary", "arbitrary"), vmem_limit_bytes=VMEM_LIMIT),
        name="mixer",
    )(x, cond3, g1, win, bin_, cw, cb, gcl, bcl, gvl, bvl, ws, bs, mg, wout, bout)


def _oddeven_merge(lo, hi, r):
    step = r * 2
    if step < hi - lo:
        yield from _oddeven_merge(lo, hi, step)
        yield from _oddeven_merge(lo + r, hi, step)
        yield from [(i, i + r) for i in range(lo + r, hi - r, step)]
    else:
        yield (lo, lo + r)


def _oddeven_merge_sort(lo, hi):
    if hi - lo >= 1:
        mid = lo + (hi - lo) // 2
        yield from _oddeven_merge_sort(lo, mid)
        yield from _oddeven_merge_sort(mid + 1, hi)
        yield from _oddeven_merge(lo, hi, 1)


SORT16 = tuple(_oddeven_merge_sort(0, TOPK - 1))
BITONIC16 = tuple((k, k + d) for d in (8, 4, 2, 1) for k in range(TOPK) if not k & d)
SUBLANES = 8


def _exchange(v, pairs):
    for i, j in pairs:
        v[i], v[j] = jnp.maximum(v[i], v[j]), jnp.minimum(v[i], v[j])
    return v


def _top16_sorted(s):
    v = _exchange([s[SUBLANES * k:SUBLANES * (k + 1), :] for k in range(TOPK)], SORT16)
    for shift in (4, 2, 1):
        v = _exchange([jnp.maximum(v[k], pltpu.roll(v[TOPK - 1 - k], shift, axis=0))
                       for k in range(TOPK)], BITONIC16)
    return v


def _top_values(work, n):
    vals = []
    for _ in range(n):
        m = jnp.max(work, axis=0, keepdims=True)
        vals.append(m)
        work = jnp.where(work == m, -jnp.inf, work)
    return vals


def _rows_of(vals, sublane):
    out = vals[SUBLANES - 1]
    for r in range(SUBLANES - 2, -1, -1):
        out = jnp.where(sublane == r, vals[r], out)
    return out


def _prep_kernel(x1_ref, cond_ref, g2_ref, wq_ref, keys_ref,
                 h2t_ref, rk_ref, e1_ref, cnt_ref, e0_ref, q_scr, s_scr):
    cond = cond_ref[0]
    sh2, sc2 = cond[3:4], cond[4:5]
    h2 = _rms(x1_ref[...]) * g2_ref[...] * (1.0 + sc2) + sh2
    h2t_ref[...] = h2.T.astype(BF16)
    q_scr[...] = jnp.dot(h2.astype(BF16), wq_ref[...], preferred_element_type=F32)

    def head(hh, carry):
        c0 = pl.multiple_of(hh * (2 * D_HALF), 2 * D_HALF)
        q0 = q_scr[:, pl.ds(c0, D_HALF)].astype(BF16)
        q1 = q_scr[:, pl.ds(c0 + D_HALF, D_HALF)].astype(BF16)
        nt = (((1,), (1,)), ((), ()))
        s_scr[0] = lax.dot_general(keys_ref[2 * hh], q0, nt, preferred_element_type=F32)
        s_scr[1] = lax.dot_general(keys_ref[2 * hh + 1], q1, nt, preferred_element_type=F32)

        def chunk(lc, carry2):
            lanes = pl.ds(pl.multiple_of(lc * LANES, LANES), LANES)
            s0 = s_scr[0, :, lanes]
            s1 = s_scr[1, :, lanes]
            a = _top16_sorted(s0)
            b = _top16_sorted(s1)
            sublane = lax.broadcasted_iota(jnp.int32, (SUBLANES, LANES), 0)
            a_lo, a_hi = _rows_of(a[:SUBLANES], sublane), _rows_of(a[SUBLANES:], sublane)
            b_hi = _rows_of(b[SUBLANES:], sublane)
            cand = [a_lo + b[0], a_hi + b[0]] + [a_lo + b[c] for c in range(1, SUBLANES)]
            cand.append(a[0] + b_hi)
            top = _top_values(jnp.concatenate(cand, axis=0), TOPK)
            tau = top[TOPK - 1]
            zsum = jnp.zeros_like(tau)
            for t in top:
                zsum = zsum + jnp.exp(t - top[0])
            rank1 = jnp.zeros(s1.shape, F32)
            count = jnp.zeros(s0.shape, F32)
            for c in range(TOPK):
                bc = jnp.concatenate([b[c]] * (N_KEYS // SUBLANES), axis=0)
                rank1 = jnp.where(bc > s1, float(c + 1), rank1)
                count = jnp.where(s0 + bc >= tau, float(c + 1), count)
            a0 = jnp.concatenate([a[0]] * (N_KEYS // SUBLANES), axis=0)
            b0 = jnp.concatenate([b[0]] * (N_KEYS // SUBLANES), axis=0)
            rk_ref[hh, :, lanes] = rank1.astype(BF16)
            e1_ref[hh, :, lanes] = (jnp.exp(s1 - b0) / zsum).astype(BF16)
            cnt_ref[hh, :, lanes] = count
            e0_ref[hh, :, lanes] = jnp.exp(s0 - a0)
            return carry2

        lax.fori_loop(0, TB // LANES, chunk, 0)
        return carry

    lax.fori_loop(0, PEER_HEADS, head, 0)


def _prep_call(x1, cond3, g2, wq, keys, tiles_per_batch):
    T, D = x1.shape
    tok = lambda: pl.BlockSpec((PEER_HEADS, N_KEYS, TB), lambda i: (0, 0, i))
    sds = lambda dt: jax.ShapeDtypeStruct((PEER_HEADS, N_KEYS, T), dt)
    return pl.pallas_call(
        _prep_kernel,
        grid=(T // TB,),
        in_specs=[pl.BlockSpec((TB, D), lambda i: (i, 0)),
                  pl.BlockSpec((1, N_COND, D), lambda i: (i // tiles_per_batch, 0, 0)),
                  pl.BlockSpec((1, D), lambda i: (0, 0)),
                  pl.BlockSpec((D, 2 * PEER_HEADS * D_HALF), lambda i: (0, 0)),
                  pl.BlockSpec((2 * PEER_HEADS, N_KEYS, D_HALF), lambda i: (0, 0, 0))],
        out_specs=[pl.BlockSpec((D, TB), lambda i: (0, i)), tok(), tok(), tok(), tok()],
        out_shape=[jax.ShapeDtypeStruct((D, T), BF16),
                   sds(BF16), sds(BF16), sds(F32), sds(F32)],
        scratch_shapes=[pltpu.VMEM((TB, 2 * PEER_HEADS * D_HALF), F32),
                        pltpu.VMEM((2, N_KEYS, TB), F32)],
        compiler_params=pltpu.CompilerParams(
            dimension_semantics=("arbitrary",), vmem_limit_bytes=VMEM_LIMIT),
        name="prep",
    )(x1, cond3, g2, wq, keys)


def _after(value, bits):
    nothing = (bits >> 16) >> 16
    return pltpu.bitcast(pltpu.bitcast(value, jnp.uint32) | nothing, BF16)


def _anchored(lhs, anchors):
    pieces = []
    for k, bits in enumerate(anchors):
        tile = lhs[:BF16_ROWS, k * MXU_K:k * MXU_K + LANES]
        pieces.append(tile if bits is None else _after(tile, bits))
        pieces.append(lhs[:BF16_ROWS, k * MXU_K + LANES:(k + 1) * MXU_K])
    return jnp.concatenate([jnp.concatenate(pieces, axis=1), lhs[BF16_ROWS:, :]], axis=0)


def _peer_kernel(h2t_ref, rk_ref, e1_ref, cnt_ref, e0_ref, u0_ref, un_ref, vt_ref,
                 x1_ref, cond_ref, gf_ref, o_ref,
                 rk_scr, e1_scr, a_cur, a_nxt, z_scr, z_prev, acc_scr):
    j = pl.program_id(1)
    nblk = pl.num_programs(1) - 1

    @pl.when(j == 0)
    def _():
        acc_scr[...] = jnp.zeros_like(acc_scr)
        z_prev[...] = jnp.zeros_like(z_prev)
        a_cur[...] = jnp.dot(u0_ref[...], h2t_ref[...], preferred_element_type=F32)
        for lc in range(TB // LANES):
            rk_scr[:, lc] = rk_ref[:, :, lc * LANES:(lc + 1) * LANES]
            e1_scr[:, lc] = e1_ref[:, :, lc * LANES:(lc + 1) * LANES]

    @pl.when(j > 0)
    def _():
        a_cur[...] = a_nxt[...]
        z_prev[...] = z_scr[...]

    @pl.when(j < nblk)
    def _():
        zero = jnp.zeros((BF16_ROWS, LANES), BF16)
        jtiles = N_KEYS // BF16_ROWS
        ktiles = D_MODEL // MXU_K
        window_tiles = WINDOW_ROWS * (TB // LANES) * jtiles
        release = window_tiles // (2 * ktiles)
        done = None
        for w in range(EB // N_KEYS // WINDOW_ROWS):
            anchors, built = [done], 0
            for ii in range(w * WINDOW_ROWS, (w + 1) * WINDOW_ROWS):
                for lc in range(TB // LANES):
                    lanes = slice(lc * LANES, (lc + 1) * LANES)
                    g = [None] * jtiles
                    for hh in range(PEER_HEADS):
                        cnt = jnp.broadcast_to(cnt_ref[hh, ii:ii + 1, lanes],
                                               (BF16_ROWS, LANES)).astype(BF16)
                        e0 = jnp.broadcast_to(e0_ref[hh, ii:ii + 1, lanes],
                                              (BF16_ROWS, LANES)).astype(BF16)
                        for q in range(jtiles):
                            rk = rk_scr[hh, lc, q * BF16_ROWS:(q + 1) * BF16_ROWS, :]
                            e1 = e1_scr[hh, lc, q * BF16_ROWS:(q + 1) * BF16_ROWS, :]
                            term = jnp.minimum(jnp.maximum(cnt - rk, zero), e0) * e1
                            g[q] = term if g[q] is None else g[q] + term
                    for q in range(jtiles):
                        r0 = ii * N_KEYS + q * BF16_ROWS
                        act = jax.nn.gelu(a_cur[r0:r0 + BF16_ROWS, lanes].astype(BF16))
                        z_tile = act * g[q]
                        z_scr[r0:r0 + BF16_ROWS, lanes] = z_tile
                        bits = pltpu.bitcast(z_tile, jnp.uint32)
                        done = bits if done is None else done | bits
                        built += 1
                        if built % release == 0 and built < window_tiles:
                            anchors.append(done)
            rows = slice(w * WINDOW_ROWS * N_KEYS, (w + 1) * WINDOW_ROWS * N_KEYS)
            a_nxt[rows, :] = jnp.dot(_anchored(un_ref[rows, :], anchors[0::2]), h2t_ref[...],
                                     preferred_element_type=F32)
            drows = slice(w * (D_MODEL // (EB // N_KEYS // WINDOW_ROWS)),
                          (w + 1) * (D_MODEL // (EB // N_KEYS // WINDOW_ROWS)))
            acc_scr[drows, :] += jnp.dot(_anchored(vt_ref[0, drows, :], anchors[1::2]), z_prev[...],
                                         preferred_element_type=F32)
    @pl.when(j == nblk)
    def _():
        acc = acc_scr[...] + jnp.dot(vt_ref[0], z_prev[...], preferred_element_type=F32)
        gt2 = cond_ref[0][5:6]
        x2 = x1_ref[...] + gt2 * acc.T
        o_ref[...] = _rms(x2) * gf_ref[...]


def _peer_call(h2t, rk, e1, cnt, e0, u_bf, vt_bf, x1, cond3, gf, tiles_per_batch):
    T, D = x1.shape
    tok = lambda: pl.BlockSpec((PEER_HEADS, N_KEYS, TB), lambda i, j: (0, 0, i))
    nblk = N_EXPERTS // EB
    last = nblk - 1
    rows = lambda: pl.BlockSpec((PEER_HEADS, EB // N_KEYS, TB),
                                lambda i, j: (0, jnp.minimum(j, last), i))
    return pl.pallas_call(
        _peer_kernel,
        grid=(T // TB, nblk + 1),
        in_specs=[pl.BlockSpec((D, TB), lambda i, j: (0, i)),
                  tok(), tok(), rows(), rows(),
                  pl.BlockSpec((EB, D), lambda i, j: (0, 0)),
                  pl.BlockSpec((EB, D), lambda i, j: (jnp.minimum(j + 1, last), 0)),
                  pl.BlockSpec((1, D, EB), lambda i, j: (jnp.maximum(j - 1, 0), 0, 0)),
                  pl.BlockSpec((TB, D), lambda i, j: (i, 0)),
                  pl.BlockSpec((1, N_COND, D), lambda i, j: (i // tiles_per_batch, 0, 0)),
                  pl.BlockSpec((1, D), lambda i, j: (0, 0))],
        out_specs=pl.BlockSpec((TB, D), lambda i, j: (i, 0)),
        out_shape=jax.ShapeDtypeStruct((T, D), F32),
        scratch_shapes=[pltpu.VMEM((PEER_HEADS, TB // LANES, N_KEYS, LANES), BF16),
                        pltpu.VMEM((PEER_HEADS, TB // LANES, N_KEYS, LANES), BF16),
                        pltpu.VMEM((EB, TB), F32),
                        pltpu.VMEM((EB, TB), F32),
                        pltpu.VMEM((EB, TB), BF16),
                        pltpu.VMEM((EB, TB), BF16),
                        pltpu.VMEM((D, TB), F32)],
        compiler_params=pltpu.CompilerParams(
            dimension_semantics=("arbitrary", "arbitrary"), vmem_limit_bytes=VMEM_LIMIT),
        name="peer",
    )(h2t, rk, e1, cnt, e0, u_bf, u_bf, vt_bf, x1, cond3, gf)


def kernel(x, c, w_ada, b_ada, g_norm1, w_in, b_in, conv_w, conv_b, g_conv_ln, b_conv_ln,
           g_v_ln, b_v_ln, w_spatial, b_spatial, w_out, b_out, g_norm2, w_query, sub_keys,
           expert_u, expert_v, g_final):
    B, S, D = x.shape
    depth = w_ada.shape[0]
    assert depth == 1, "the final rmsnorm is fused into the PEER call of the only layer"
    T = B * S
    row = lambda v: v.reshape(1, -1)
    grp = jnp.arange(D_CONV) // GROUP
    mg = jnp.where(grp[:, None] == grp[None, :], 1.0 / GROUP, 0.0).astype(BF16)
    c_pad = jnp.pad(c, ((0, 8 - B), (0, 0)))
    for l in range(depth):
        cond = _cond_call(c_pad, w_ada[l], row(b_ada[l]))
        cond3 = cond[:B].reshape(B, N_COND, D)
        bs = jnp.repeat(b_spatial[l].T, GROUP, axis=1)
        x1 = _mixer_call(x, cond3, row(g_norm1[l]), w_in[l].astype(BF16), row(b_in[l]),
                         conv_w[l], row(conv_b[l]), row(g_conv_ln[l]), row(b_conv_ln[l]),
                         row(g_v_ln[l]), row(b_v_ln[l]), w_spatial[l], bs, mg,
                         w_out[l].astype(BF16), row(b_out[l]))
        x1 = x1.reshape(T, D)
        keys = sub_keys[l].reshape(2 * PEER_HEADS, N_KEYS, D_HALF).astype(BF16)
        h2t, rk, e1, cnt, e0 = _prep_call(x1, cond3, row(g_norm2[l]), w_query[l].astype(BF16),
                                          keys, S // TB)
        vt = jnp.swapaxes(expert_v[l].astype(BF16).reshape(N_EXPERTS // EB, EB, D), 1, 2)
        x = _peer_call(h2t, rk, e1, cnt, e0, expert_u[l].astype(BF16), vt, x1, cond3,
                       row(g_final), S // TB).reshape(B, S, D)
    return x
```

```python
import jax
import jax.numpy as jnp
from jax import lax
from jax.experimental import pallas as pl
from jax.experimental.pallas import tpu as pltpu

F32 = jnp.float32
BF16 = jnp.bfloat16

D_MODEL = 1024
D_CONV = 512
D_GMLP = 512
GROUP = 64
N_HEADS_G = 8
CONV_WIDTH = 31
CHUNK = 128
N_PROJ = 2048
N_KEYS = 128
N_EXPERTS = N_KEYS * N_KEYS
PEER_HEADS = 8
D_HALF = 128
TOPK = 16
N_COND = 6
EPS = 1e-6
GELU_C1 = 0.7978845608028654
GELU_C2 = 0.044715 * GELU_C1

V7X_VMEM_BYTES = 64 * 1024 * 1024
VMEM_LIMIT = V7X_VMEM_BYTES * 3 // 4
SUBLANES = 8
LANES = 128
BF16_ROWS = 16

TS = 512
CONV_HALO = 32
CONV_ROWS = 64
TB = 512
EB = 1024


def _rms(x):
    return x * lax.rsqrt(jnp.mean(x * x, axis=-1, keepdims=True) + EPS)


def _split_dot(x, m):
    hi = x.astype(BF16)
    lo = (x - hi.astype(F32)).astype(BF16)
    return (jnp.dot(hi, m, preferred_element_type=F32)
            + jnp.dot(lo, m, preferred_element_type=F32))


def _group_ln(y, mg, g, b):
    mu = _split_dot(y, mg)
    d = y - mu
    var = _split_dot(d * d, mg)
    return d * lax.rsqrt(var + EPS) * g + b


def _cond_kernel(c_ref, w_ref, b_ref, o_ref):
    c = c_ref[...]
    s = c * jax.nn.sigmoid(c)
    o_ref[...] = jnp.dot(s, w_ref[...], preferred_element_type=F32,
                         precision=lax.Precision.HIGHEST) + b_ref[...]


def _cond_call(c_pad, w_ada, b_ada):
    n = w_ada.shape[1]
    blk = 1024
    return pl.pallas_call(
        _cond_kernel,
        grid=(n // blk,),
        in_specs=[pl.BlockSpec((8, D_MODEL), lambda j: (0, 0)),
                  pl.BlockSpec((D_MODEL, blk), lambda j: (0, j)),
                  pl.BlockSpec((1, blk), lambda j: (0, j))],
        out_specs=pl.BlockSpec((8, blk), lambda j: (0, j)),
        out_shape=jax.ShapeDtypeStruct((8, n), F32),
        compiler_params=pltpu.CompilerParams(
            dimension_semantics=("arbitrary",), vmem_limit_bytes=VMEM_LIMIT),
        name="cond",
    )(c_pad, w_ada, b_ada)


def _mixer_kernel(x_ref, cond_ref, g1_ref, win_ref, bin_ref, cw_ref, cb_ref,
                  gcl_ref, bcl_ref, gvl_ref, bvl_ref, ws_ref, bs_ref, mg_ref,
                  wout_ref, bout_ref, x1_ref, abuf, ybuf):
    s = pl.program_id(1)
    x = x_ref[0]
    cond = cond_ref[0]
    sh1, sc1, gt1 = cond[0:1], cond[1:2], cond[2:3]
    h = _rms(x) * g1_ref[...] * (1.0 + sc1) + sh1
    p = jnp.dot(h.astype(BF16), win_ref[...], preferred_element_type=F32) + bin_ref[...]

    a = p[:, :D_CONV] * jax.nn.sigmoid(p[:, D_CONV:2 * D_CONV])

    @pl.when(s == 0)
    def _():
        abuf[0:CONV_HALO, :] = jnp.zeros((CONV_HALO, D_CONV), F32)

    @pl.when(s > 0)
    def _():
        abuf[0:CONV_HALO, :] = abuf[TS:TS + CONV_HALO, :]

    abuf[CONV_HALO:CONV_HALO + TS, :] = a

    off = CONV_HALO - (CONV_WIDTH - 1)
    win_rows = CONV_ROWS + CONV_HALO
    for r in range(TS // CONV_ROWS):
        base = r * CONV_ROWS
        window = abuf[base:base + win_rows, :]
        shifted = {0: window}
        acc = jnp.zeros((CONV_ROWS, D_CONV), F32) + cb_ref[...]
        for k in range(CONV_WIDTH):
            res = (off + k) % SUBLANES
            lo = off + k - res
            if res not in shifted:
                shifted[res] = pltpu.roll(window, win_rows - res, axis=0)
            acc = acc + shifted[res][lo:lo + CONV_ROWS, :] * cw_ref[k:k + 1, :]
        ybuf[base:base + CONV_ROWS, :] = acc
    mg = mg_ref[...]
    ya = _group_ln(ybuf[...], mg, gcl_ref[...], bcl_ref[...])
    ya = ya * jax.nn.sigmoid(ya)

    z = jax.nn.gelu(p[:, 2 * D_CONV:])
    u = z[:, :D_GMLP]
    v = _group_ln(z[:, D_GMLP:], mg, gvl_ref[...], bvl_ref[...])
    row = lax.broadcasted_iota(jnp.int32, (CHUNK, CHUNK), 0)
    col = lax.broadcasted_iota(jnp.int32, (CHUNK, CHUNK), 1)
    lane_head = lax.broadcasted_iota(jnp.int32, (CHUNK, D_GMLP), 1) // GROUP
    w_heads = [jnp.where(row >= col, ws_ref[hh], 0.0).astype(BF16) for hh in range(N_HEADS_G)]
    yb_chunks = []
    for ch in range(TS // CHUNK):
        vc = v[ch * CHUNK:(ch + 1) * CHUNK, :]
        mixed = bs_ref[...]
        for hh in range(N_HEADS_G):
            vm = jnp.where(lane_head == hh, vc, 0.0).astype(BF16)
            mixed = mixed + jnp.dot(w_heads[hh], vm, preferred_element_type=F32)
        yb_chunks.append(u[ch * CHUNK:(ch + 1) * CHUNK, :] * mixed)
    yb = jnp.concatenate(yb_chunks, axis=0)

    y = (jnp.dot(ya.astype(BF16), wout_ref[0:D_CONV, :], preferred_element_type=F32)
         + jnp.dot(yb.astype(BF16), wout_ref[D_CONV:, :], preferred_element_type=F32)
         + bout_ref[...])
    x1_ref[0] = x + gt1 * y


def _mixer_call(x, cond3, g1, win, bin_, cw, cb, gcl, bcl, gvl, bvl, ws, bs, mg, wout, bout):
    B, S, D = x.shape
    full = lambda shape: pl.BlockSpec(shape, lambda b, s: (0,) * len(shape))
    return pl.pallas_call(
        _mixer_kernel,
        grid=(B, S // TS),
        in_specs=[pl.BlockSpec((1, TS, D), lambda b, s: (b, s, 0)),
                  pl.BlockSpec((1, N_COND, D), lambda b, s: (b, 0, 0)),
                  full((1, D)), full((D, N_PROJ)), full((1, N_PROJ)),
                  full((CONV_WIDTH, D_CONV)), full((1, D_CONV)),
                  full((1, D_CONV)), full((1, D_CONV)), full((1, D_GMLP)), full((1, D_GMLP)),
                  full((N_HEADS_G, CHUNK, CHUNK)), full((CHUNK, D_GMLP)),
                  full((D_CONV, D_CONV)), full((D, D)), full((1, D))],
        out_specs=pl.BlockSpec((1, TS, D), lambda b, s: (b, s, 0)),
        out_shape=jax.ShapeDtypeStruct((B, S, D), F32),
        scratch_shapes=[pltpu.VMEM((TS + CONV_HALO, D_CONV), F32),
                        pltpu.VMEM((TS, D_CONV), F32)],
        compiler_params=pltpu.CompilerParams(
            dimension_semantics=("arbitrary", "arbitrary"), vmem_limit_bytes=VMEM_LIMIT),
        name="mixer",
    )(x, cond3, g1, win, bin_, cw, cb, gcl, bcl, gvl, bvl, ws, bs, mg, wout, bout)


def _oddeven_merge(lo, hi, r):
    step = r * 2
    if step < hi - lo:
        yield from _oddeven_merge(lo, hi, step)
        yield from _oddeven_merge(lo + r, hi, step)
        yield from [(i, i + r) for i in range(lo + r, hi - r, step)]
    else:
        yield (lo, lo + r)


def _oddeven_merge_sort(lo, hi):
    if hi - lo >= 1:
        mid = lo + (hi - lo) // 2
        yield from _oddeven_merge_sort(lo, mid)
        yield from _oddeven_merge_sort(mid + 1, hi)
        yield from _oddeven_merge(lo, hi, 1)


SORT16 = tuple(_oddeven_merge_sort(0, TOPK - 1))
BITONIC16 = tuple((k, k + d) for d in (8, 4, 2, 1) for k in range(TOPK) if not k & d)


def _exchange(v, pairs):
    for i, j in pairs:
        v[i], v[j] = jnp.maximum(v[i], v[j]), jnp.minimum(v[i], v[j])
    return v


def _top16_sorted(s):
    v = _exchange([s[SUBLANES * k:SUBLANES * (k + 1), :] for k in range(TOPK)], SORT16)
    for shift in (4, 2, 1):
        v = _exchange([jnp.maximum(v[k], pltpu.roll(v[TOPK - 1 - k], shift, axis=0))
                       for k in range(TOPK)], BITONIC16)
    return v


def _top_values(work, n):
    vals = []
    for _ in range(n):
        m = jnp.max(work, axis=0, keepdims=True)
        vals.append(m)
        work = jnp.where(work == m, -jnp.inf, work)
    return vals


def _rows_of(vals, sublane):
    out = vals[SUBLANES - 1]
    for r in range(SUBLANES - 2, -1, -1):
        out = jnp.where(sublane == r, vals[r], out)
    return out


def _prep_kernel(x1_ref, cond_ref, g2_ref, wq_ref, keys_ref,
                 h2t_ref, rk_ref, e1_ref, cnt_ref, e0_ref, q_scr, s_scr):
    cond = cond_ref[0]
    sh2, sc2 = cond[3:4], cond[4:5]
    h2 = _rms(x1_ref[...]) * g2_ref[...] * (1.0 + sc2) + sh2
    h2t_ref[...] = h2.T.astype(BF16)
    q_scr[...] = jnp.dot(h2.astype(BF16), wq_ref[...], preferred_element_type=F32)

    def head(hh, carry):
        c0 = pl.multiple_of(hh * (2 * D_HALF), 2 * D_HALF)
        q0 = q_scr[:, pl.ds(c0, D_HALF)].astype(BF16)
        q1 = q_scr[:, pl.ds(c0 + D_HALF, D_HALF)].astype(BF16)
        nt = (((1,), (1,)), ((), ()))
        s_scr[0] = lax.dot_general(keys_ref[2 * hh], q0, nt, preferred_element_type=F32)
        s_scr[1] = lax.dot_general(keys_ref[2 * hh + 1], q1, nt, preferred_element_type=F32)

        def chunk(lc, carry2):
            lanes = pl.ds(pl.multiple_of(lc * LANES, LANES), LANES)
            s0 = s_scr[0, :, lanes]
            s1 = s_scr[1, :, lanes]
            a = _top16_sorted(s0)
            b = _top16_sorted(s1)
            sublane = lax.broadcasted_iota(jnp.int32, (SUBLANES, LANES), 0)
            a_lo, a_hi = _rows_of(a[:SUBLANES], sublane), _rows_of(a[SUBLANES:], sublane)
            b_hi = _rows_of(b[SUBLANES:], sublane)
            cand = [a_lo + b[0], a_hi + b[0]] + [a_lo + b[c] for c in range(1, SUBLANES)]
            cand.append(a[0] + b_hi)
            top = _top_values(jnp.concatenate(cand, axis=0), TOPK)
            tau = top[TOPK - 1]
            zsum = jnp.zeros_like(tau)
            for t in top:
                zsum = zsum + jnp.exp(t - top[0])
            rank1 = jnp.zeros(s1.shape, F32)
            count = jnp.zeros(s0.shape, F32)
            for c in range(TOPK):
                bc = jnp.concatenate([b[c]] * (N_KEYS // SUBLANES), axis=0)
                rank1 = jnp.where(bc > s1, float(c + 1), rank1)
                count = jnp.where(s0 + bc >= tau, float(c + 1), count)
            a0 = jnp.concatenate([a[0]] * (N_KEYS // SUBLANES), axis=0)
            b0 = jnp.concatenate([b[0]] * (N_KEYS // SUBLANES), axis=0)
            rk_ref[hh, :, lanes] = rank1.astype(BF16)
            e1_ref[hh, :, lanes] = (jnp.exp(s1 - b0) * (0.5 / zsum)).astype(BF16)
            cnt_ref[hh, :, lanes] = count
            e0_ref[hh, :, lanes] = jnp.exp(s0 - a0)
            return carry2

        lax.fori_loop(0, TB // LANES, chunk, 0)
        return carry

    lax.fori_loop(0, PEER_HEADS, head, 0)


def _prep_call(x1, cond3, g2, wq, keys, tiles_per_batch):
    T, D = x1.shape
    tok = lambda: pl.BlockSpec((PEER_HEADS, N_KEYS, TB), lambda i: (0, 0, i))
    sds = lambda dt: jax.ShapeDtypeStruct((PEER_HEADS, N_KEYS, T), dt)
    return pl.pallas_call(
        _prep_kernel,
        grid=(T // TB,),
        in_specs=[pl.BlockSpec((TB, D), lambda i: (i, 0)),
                  pl.BlockSpec((1, N_COND, D), lambda i: (i // tiles_per_batch, 0, 0)),
                  pl.BlockSpec((1, D), lambda i: (0, 0)),
                  pl.BlockSpec((D, 2 * PEER_HEADS * D_HALF), lambda i: (0, 0)),
                  pl.BlockSpec((2 * PEER_HEADS, N_KEYS, D_HALF), lambda i: (0, 0, 0))],
        out_specs=[pl.BlockSpec((D, TB), lambda i: (0, i)), tok(), tok(), tok(), tok()],
        out_shape=[jax.ShapeDtypeStruct((D, T), BF16),
                   sds(BF16), sds(BF16), sds(F32), sds(F32)],
        scratch_shapes=[pltpu.VMEM((TB, 2 * PEER_HEADS * D_HALF), F32),
                        pltpu.VMEM((2, N_KEYS, TB), F32)],
        compiler_params=pltpu.CompilerParams(
            dimension_semantics=("arbitrary",), vmem_limit_bytes=VMEM_LIMIT),
        name="prep",
    )(x1, cond3, g2, wq, keys)


def _peer_kernel(h2t_ref, rk_ref, e1_ref, cnt_ref, e0_ref, u_ref, vt_ref,
                 x1_ref, cond_ref, gf_ref, o_ref, rk_scr, e1_scr, a_scr, z_scr, acc_scr):
    j = pl.program_id(1)

    @pl.when(j == 0)
    def _():
        acc_scr[...] = jnp.zeros_like(acc_scr)
        for lc in range(TB // LANES):
            rk_scr[:, lc] = rk_ref[:, :, lc * LANES:(lc + 1) * LANES]
            e1_scr[:, lc] = e1_ref[:, :, lc * LANES:(lc + 1) * LANES]

    a_scr[...] = jnp.dot(u_ref[...], h2t_ref[...], preferred_element_type=F32)

    zero = jnp.zeros((BF16_ROWS, LANES), BF16)
    jtiles = N_KEYS // BF16_ROWS
    for ii in range(EB // N_KEYS):
        for lc in range(TB // LANES):
            lanes = slice(lc * LANES, (lc + 1) * LANES)
            g = [None] * jtiles
            for hh in range(PEER_HEADS):
                cnt = jnp.broadcast_to(cnt_ref[hh, ii:ii + 1, lanes], (BF16_ROWS, LANES)).astype(BF16)
                e0 = jnp.broadcast_to(e0_ref[hh, ii:ii + 1, lanes], (BF16_ROWS, LANES)).astype(BF16)
                for q in range(jtiles):
                    rk = rk_scr[hh, lc, q * BF16_ROWS:(q + 1) * BF16_ROWS, :]
                    e1 = e1_scr[hh, lc, q * BF16_ROWS:(q + 1) * BF16_ROWS, :]
                    term = jnp.minimum(jnp.maximum(cnt - rk, zero), e0) * e1
                    g[q] = term if g[q] is None else g[q] + term
            for q in range(jtiles):
                r0 = ii * N_KEYS + q * BF16_ROWS
                xb = a_scr[r0:r0 + BF16_ROWS, lanes].astype(BF16)
                th = jnp.tanh(xb * (GELU_C1 + GELU_C2 * (xb * xb)))
                z_scr[r0:r0 + BF16_ROWS, lanes] = (xb * th + xb) * g[q]
    acc_scr[...] += jnp.dot(vt_ref[0], z_scr[...], preferred_element_type=F32)

    @pl.when(j == pl.num_programs(1) - 1)
    def _():
        gt2 = cond_ref[0][5:6]
        x2 = x1_ref[...] + gt2 * acc_scr[...].T
        o_ref[...] = _rms(x2) * gf_ref[...]


def _peer_call(h2t, rk, e1, cnt, e0, u_bf, vt_bf, x1, cond3, gf, tiles_per_batch):
    T, D = x1.shape
    tok = lambda: pl.BlockSpec((PEER_HEADS, N_KEYS, TB), lambda i, j: (0, 0, i))
    rows = lambda: pl.BlockSpec((PEER_HEADS, EB // N_KEYS, TB), lambda i, j: (0, j, i))
    return pl.pallas_call(
        _peer_kernel,
        grid=(T // TB, N_EXPERTS // EB),
        in_specs=[pl.BlockSpec((D, TB), lambda i, j: (0, i)),
                  tok(), tok(), rows(), rows(),
                  pl.BlockSpec((EB, D), lambda i, j: (j, 0)),
                  pl.BlockSpec((1, D, EB), lambda i, j: (j, 0, 0)),
                  pl.BlockSpec((TB, D), lambda i, j: (i, 0)),
                  pl.BlockSpec((1, N_COND, D), lambda i, j: (i // tiles_per_batch, 0, 0)),
                  pl.BlockSpec((1, D), lambda i, j: (0, 0))],
        out_specs=pl.BlockSpec((TB, D), lambda i, j: (i, 0)),
        out_shape=jax.ShapeDtypeStruct((T, D), F32),
        scratch_shapes=[pltpu.VMEM((PEER_HEADS, TB // LANES, N_KEYS, LANES), BF16),
                        pltpu.VMEM((PEER_HEADS, TB // LANES, N_KEYS, LANES), BF16),
                        pltpu.VMEM((EB, TB), F32),
                        pltpu.VMEM((EB, TB), BF16),
                        pltpu.VMEM((D, TB), F32)],
        compiler_params=pltpu.CompilerParams(
            dimension_semantics=("arbitrary", "arbitrary"), vmem_limit_bytes=VMEM_LIMIT),
        name="peer",
    )(h2t, rk, e1, cnt, e0, u_bf, vt_bf, x1, cond3, gf)


def kernel(x, c, w_ada, b_ada, g_norm1, w_in, b_in, conv_w, conv_b, g_conv_ln, b_conv_ln,
           g_v_ln, b_v_ln, w_spatial, b_spatial, w_out, b_out, g_norm2, w_query, sub_keys,
           expert_u, expert_v, g_final):
    B, S, D = x.shape
    depth = w_ada.shape[0]
    assert depth == 1, "the final rmsnorm is fused into the PEER call of the only layer"
    T = B * S
    row = lambda v: v.reshape(1, -1)
    grp = jnp.arange(D_CONV) // GROUP
    mg = jnp.where(grp[:, None] == grp[None, :], 1.0 / GROUP, 0.0).astype(BF16)
    c_pad = jnp.pad(c, ((0, 8 - B), (0, 0)))
    for l in range(depth):
        cond = _cond_call(c_pad, w_ada[l], row(b_ada[l]))
        cond3 = cond[:B].reshape(B, N_COND, D)
        bs = jnp.repeat(b_spatial[l].T, GROUP, axis=1)
        x1 = _mixer_call(x, cond3, row(g_norm1[l]), w_in[l].astype(BF16), row(b_in[l]),
                         conv_w[l], row(conv_b[l]), row(g_conv_ln[l]), row(b_conv_ln[l]),
                         row(g_v_ln[l]), row(b_v_ln[l]), w_spatial[l], bs, mg,
                         w_out[l].astype(BF16), row(b_out[l]))
        x1 = x1.reshape(T, D)
        keys = sub_keys[l].reshape(2 * PEER_HEADS, N_KEYS, D_HALF).astype(BF16)
        h2t, rk, e1, cnt, e0 = _prep_call(x1, cond3, row(g_norm2[l]), w_query[l].astype(BF16),
                                          keys, S // TB)
        vt = jnp.swapaxes(expert_v[l].astype(BF16).reshape(N_EXPERTS // EB, EB, D), 1, 2)
        x = _peer_call(h2t, rk, e1, cnt, e0, expert_u[l].astype(BF16), vt, x1, cond3,
                       row(g_final), S // TB).reshape(B, S, D)
    return x
```

```python
import jax
import jax.numpy as jnp
from jax import lax
from jax.experimental import pallas as pl
from jax.experimental.pallas import tpu as pltpu

F32 = jnp.float32
BF16 = jnp.bfloat16

D_MODEL = 1024
D_CONV = 512
D_GMLP = 512
GROUP = 64
N_HEADS_G = 8
CONV_WIDTH = 31
CHUNK = 128
N_PROJ = 2048
N_KEYS = 128
N_EXPERTS = N_KEYS * N_KEYS
PEER_HEADS = 8
D_HALF = 128
TOPK = 16
N_COND = 6
EPS = 1e-6
GELU_C1 = 0.7978845608028654
GELU_C2 = 0.044715 * GELU_C1

V7X_VMEM_BYTES = 64 * 1024 * 1024
VMEM_LIMIT = V7X_VMEM_BYTES * 3 // 4
SUBLANES = 8
LANES = 128
BF16_ROWS = 16

TS = 512
CONV_HALO = 32
CONV_ROWS = 64
TB = 512
EB = 2048


def _rms(x):
    return x * lax.rsqrt(jnp.mean(x * x, axis=-1, keepdims=True) + EPS)


def _split_dot(x, m):
    hi = x.astype(BF16)
    lo = (x - hi.astype(F32)).astype(BF16)
    return (jnp.dot(hi, m, preferred_element_type=F32)
            + jnp.dot(lo, m, preferred_element_type=F32))


def _group_ln(y, mg, g, b):
    mu = _split_dot(y, mg)
    d = y - mu
    var = _split_dot(d * d, mg)
    return d * lax.rsqrt(var + EPS) * g + b


def _cond_kernel(c_ref, w_ref, b_ref, o_ref):
    c = c_ref[...]
    s = c * jax.nn.sigmoid(c)
    o_ref[...] = jnp.dot(s, w_ref[...], preferred_element_type=F32,
                         precision=lax.Precision.HIGHEST) + b_ref[...]


def _cond_call(c_pad, w_ada, b_ada):
    n = w_ada.shape[1]
    blk = 1024
    return pl.pallas_call(
        _cond_kernel,
        grid=(n // blk,),
        in_specs=[pl.BlockSpec((8, D_MODEL), lambda j: (0, 0)),
                  pl.BlockSpec((D_MODEL, blk), lambda j: (0, j)),
                  pl.BlockSpec((1, blk), lambda j: (0, j))],
        out_specs=pl.BlockSpec((8, blk), lambda j: (0, j)),
        out_shape=jax.ShapeDtypeStruct((8, n), F32),
        compiler_params=pltpu.CompilerParams(
            dimension_semantics=("arbitrary",), vmem_limit_bytes=VMEM_LIMIT),
        name="cond",
    )(c_pad, w_ada, b_ada)


def _mixer_kernel(x_ref, cond_ref, g1_ref, win_ref, bin_ref, cw_ref, cb_ref,
                  gcl_ref, bcl_ref, gvl_ref, bvl_ref, ws_ref, bs_ref, mg_ref,
                  wout_ref, bout_ref, x1_ref, abuf, ybuf):
    s = pl.program_id(1)
    x = x_ref[0]
    cond = cond_ref[0]
    sh1, sc1, gt1 = cond[0:1], cond[1:2], cond[2:3]
    h = _rms(x) * g1_ref[...] * (1.0 + sc1) + sh1
    p = jnp.dot(h.astype(BF16), win_ref[...], preferred_element_type=F32) + bin_ref[...]

    a = p[:, :D_CONV] * jax.nn.sigmoid(p[:, D_CONV:2 * D_CONV])

    @pl.when(s == 0)
    def _():
        abuf[0:CONV_HALO, :] = jnp.zeros((CONV_HALO, D_CONV), F32)

    @pl.when(s > 0)
    def _():
        abuf[0:CONV_HALO, :] = abuf[TS:TS + CONV_HALO, :]

    abuf[CONV_HALO:CONV_HALO + TS, :] = a

    off = CONV_HALO - (CONV_WIDTH - 1)
    win_rows = CONV_ROWS + CONV_HALO
    for r in range(TS // CONV_ROWS):
        base = r * CONV_ROWS
        window = abuf[base:base + win_rows, :]
        shifted = {0: window}
        acc = jnp.zeros((CONV_ROWS, D_CONV), F32) + cb_ref[...]
        for k in range(CONV_WIDTH):
            res = (off + k) % SUBLANES
            lo = off + k - res
            if res not in shifted:
                shifted[res] = pltpu.roll(window, win_rows - res, axis=0)
            acc = acc + shifted[res][lo:lo + CONV_ROWS, :] * cw_ref[k:k + 1, :]
        ybuf[base:base + CONV_ROWS, :] = acc
    mg = mg_ref[...]
    ya = _group_ln(ybuf[...], mg, gcl_ref[...], bcl_ref[...])
    ya = ya * jax.nn.sigmoid(ya)

    z = jax.nn.gelu(p[:, 2 * D_CONV:])
    u = z[:, :D_GMLP]
    v = _group_ln(z[:, D_GMLP:], mg, gvl_ref[...], bvl_ref[...])
    row = lax.broadcasted_iota(jnp.int32, (CHUNK, CHUNK), 0)
    col = lax.broadcasted_iota(jnp.int32, (CHUNK, CHUNK), 1)
    lane_head = lax.broadcasted_iota(jnp.int32, (CHUNK, D_GMLP), 1) // GROUP
    w_heads = [jnp.where(row >= col, ws_ref[hh], 0.0).astype(BF16) for hh in range(N_HEADS_G)]
    yb_chunks = []
    for ch in range(TS // CHUNK):
        vc = v[ch * CHUNK:(ch + 1) * CHUNK, :]
        mixed = bs_ref[...]
        for hh in range(N_HEADS_G):
            vm = jnp.where(lane_head == hh, vc, 0.0).astype(BF16)
            mixed = mixed + jnp.dot(w_heads[hh], vm, preferred_element_type=F32)
        yb_chunks.append(u[ch * CHUNK:(ch + 1) * CHUNK, :] * mixed)
    yb = jnp.concatenate(yb_chunks, axis=0)

    y = (jnp.dot(ya.astype(BF16), wout_ref[0:D_CONV, :], preferred_element_type=F32)
         + jnp.dot(yb.astype(BF16), wout_ref[D_CONV:, :], preferred_element_type=F32)
         + bout_ref[...])
    x1_ref[0] = x + gt1 * y


def _mixer_call(x, cond3, g1, win, bin_, cw, cb, gcl, bcl, gvl, bvl, ws, bs, mg, wout, bout):
    B, S, D = x.shape
    full = lambda shape: pl.BlockSpec(shape, lambda b, s: (0,) * len(shape))
    return pl.pallas_call(
        _mixer_kernel,
        grid=(B, S // TS),
        in_specs=[pl.BlockSpec((1, TS, D), lambda b, s: (b, s, 0)),
                  pl.BlockSpec((1, N_COND, D), lambda b, s: (b, 0, 0)),
                  full((1, D)), full((D, N_PROJ)), full((1, N_PROJ)),
                  full((CONV_WIDTH, D_CONV)), full((1, D_CONV)),
                  full((1, D_CONV)), full((1, D_CONV)), full((1, D_GMLP)), full((1, D_GMLP)),
                  full((N_HEADS_G, CHUNK, CHUNK)), full((CHUNK, D_GMLP)),
                  full((D_CONV, D_CONV)), full((D, D)), full((1, D))],
        out_specs=pl.BlockSpec((1, TS, D), lambda b, s: (b, s, 0)),
        out_shape=jax.ShapeDtypeStruct((B, S, D), F32),
        scratch_shapes=[pltpu.VMEM((TS + CONV_HALO, D_CONV), F32),
                        pltpu.VMEM((TS, D_CONV), F32)],
        compiler_params=pltpu.CompilerParams(
            dimension_semantics=("arbitrary", "arbitrary"), vmem_limit_bytes=VMEM_LIMIT),
        name="mixer",
    )(x, cond3, g1, win, bin_, cw, cb, gcl, bcl, gvl, bvl, ws, bs, mg, wout, bout)


def _oddeven_merge(lo, hi, r):
    step = r * 2
    if step < hi - lo:
        yield from _oddeven_merge(lo, hi, step)
        yield from _oddeven_merge(lo + r, hi, step)
        yield from [(i, i + r) for i in range(lo + r, hi - r, step)]
    else:
        yield (lo, lo + r)


def _oddeven_merge_sort(lo, hi):
    if hi - lo >= 1:
        mid = lo + (hi - lo) // 2
        yield from _oddeven_merge_sort(lo, mid)
        yield from _oddeven_merge_sort(mid + 1, hi)
        yield from _oddeven_merge(lo, hi, 1)


SORT16 = tuple(_oddeven_merge_sort(0, TOPK - 1))
BITONIC16 = tuple((k, k + d) for d in (8, 4, 2, 1) for k in range(TOPK) if not k & d)


def _exchange(v, pairs):
    for i, j in pairs:
        v[i], v[j] = jnp.maximum(v[i], v[j]), jnp.minimum(v[i], v[j])
    return v


def _top16_sorted(s):
    v = _exchange([s[SUBLANES * k:SUBLANES * (k + 1), :] for k in range(TOPK)], SORT16)
    for shift in (4, 2, 1):
        v = _exchange([jnp.maximum(v[k], pltpu.roll(v[TOPK - 1 - k], shift, axis=0))
                       for k in range(TOPK)], BITONIC16)
    return v


def _top_values(work, n):
    vals = []
    for _ in range(n):
        m = jnp.max(work, axis=0, keepdims=True)
        vals.append(m)
        work = jnp.where(work == m, -jnp.inf, work)
    return vals


def _rows_of(vals, sublane):
    out = vals[SUBLANES - 1]
    for r in range(SUBLANES - 2, -1, -1):
        out = jnp.where(sublane == r, vals[r], out)
    return out


def _prep_kernel(x1_ref, cond_ref, g2_ref, wq_ref, keys_ref,
                 h2t_ref, rk_ref, e1_ref, cnt_ref, e0_ref, q_scr, s_scr):
    cond = cond_ref[0]
    sh2, sc2 = cond[3:4], cond[4:5]
    h2 = _rms(x1_ref[...]) * g2_ref[...] * (1.0 + sc2) + sh2
    h2t_ref[...] = h2.T.astype(BF16)
    q_scr[...] = jnp.dot(h2.astype(BF16), wq_ref[...], preferred_element_type=F32)

    def head(hh, carry):
        c0 = pl.multiple_of(hh * (2 * D_HALF), 2 * D_HALF)
        q0 = q_scr[:, pl.ds(c0, D_HALF)].astype(BF16)
        q1 = q_scr[:, pl.ds(c0 + D_HALF, D_HALF)].astype(BF16)
        nt = (((1,), (1,)), ((), ()))
        s_scr[0] = lax.dot_general(keys_ref[2 * hh], q0, nt, preferred_element_type=F32)
        s_scr[1] = lax.dot_general(keys_ref[2 * hh + 1], q1, nt, preferred_element_type=F32)

        def chunk(lc, carry2):
            lanes = pl.ds(pl.multiple_of(lc * LANES, LANES), LANES)
            s0 = s_scr[0, :, lanes]
            s1 = s_scr[1, :, lanes]
            a = _top16_sorted(s0)
            b = _top16_sorted(s1)
            sublane = lax.broadcasted_iota(jnp.int32, (SUBLANES, LANES), 0)
            a_lo, a_hi = _rows_of(a[:SUBLANES], sublane), _rows_of(a[SUBLANES:], sublane)
            b_hi = _rows_of(b[SUBLANES:], sublane)
            cand = [a_lo + b[0], a_hi + b[0]] + [a_lo + b[c] for c in range(1, SUBLANES)]
            cand.append(a[0] + b_hi)
            top = _top_values(jnp.concatenate(cand, axis=0), TOPK)
            tau = top[TOPK - 1]
            zsum = jnp.zeros_like(tau)
            for t in top:
                zsum = zsum + jnp.exp(t - top[0])
            rank1 = jnp.zeros(s1.shape, F32)
            count = jnp.zeros(s0.shape, F32)
            for c in range(TOPK):
                bc = jnp.concatenate([b[c]] * (N_KEYS // SUBLANES), axis=0)
                rank1 = jnp.where(bc > s1, float(c + 1), rank1)
                count = jnp.where(s0 + bc >= tau, float(c + 1), count)
            a0 = jnp.concatenate([a[0]] * (N_KEYS // SUBLANES), axis=0)
            b0 = jnp.concatenate([b[0]] * (N_KEYS // SUBLANES), axis=0)
            rk_ref[hh, :, lanes] = rank1.astype(BF16)
            e1_ref[hh, :, lanes] = (jnp.exp(s1 - b0) * (0.5 / zsum)).astype(BF16)
            cnt_ref[hh, :, lanes] = count
            e0_ref[hh, :, lanes] = jnp.exp(s0 - a0)
            return carry2

        lax.fori_loop(0, TB // LANES, chunk, 0)
        return carry

    lax.fori_loop(0, PEER_HEADS, head, 0)


def _prep_call(x1, cond3, g2, wq, keys, tiles_per_batch):
    T, D = x1.shape
    tok = lambda: pl.BlockSpec((PEER_HEADS, N_KEYS, TB), lambda i: (0, 0, i))
    sds = lambda dt: jax.ShapeDtypeStruct((PEER_HEADS, N_KEYS, T), dt)
    return pl.pallas_call(
        _prep_kernel,
        grid=(T // TB,),
        in_specs=[pl.BlockSpec((TB, D), lambda i: (i, 0)),
                  pl.BlockSpec((1, N_COND, D), lambda i: (i // tiles_per_batch, 0, 0)),
                  pl.BlockSpec((1, D), lambda i: (0, 0)),
                  pl.BlockSpec((D, 2 * PEER_HEADS * D_HALF), lambda i: (0, 0)),
                  pl.BlockSpec((2 * PEER_HEADS, N_KEYS, D_HALF), lambda i: (0, 0, 0))],
        out_specs=[pl.BlockSpec((D, TB), lambda i: (0, i)), tok(), tok(), tok(), tok()],
        out_shape=[jax.ShapeDtypeStruct((D, T), BF16),
                   sds(BF16), sds(BF16), sds(F32), sds(F32)],
        scratch_shapes=[pltpu.VMEM((TB, 2 * PEER_HEADS * D_HALF), F32),
                        pltpu.VMEM((2, N_KEYS, TB), F32)],
        compiler_params=pltpu.CompilerParams(
            dimension_semantics=("arbitrary",), vmem_limit_bytes=VMEM_LIMIT),
        name="prep",
    )(x1, cond3, g2, wq, keys)


def _peer_kernel(h2t_ref, rk_ref, e1_ref, cnt_ref, e0_ref, u_ref, vt_ref,
                 x1_ref, cond_ref, gf_ref, o_ref, rk_scr, e1_scr, a_scr, z_scr, acc_scr):
    j = pl.program_id(1)

    @pl.when(j == 0)
    def _():
        acc_scr[...] = jnp.zeros_like(acc_scr)
        for lc in range(TB // LANES):
            rk_scr[:, lc] = rk_ref[:, :, lc * LANES:(lc + 1) * LANES]
            e1_scr[:, lc] = e1_ref[:, :, lc * LANES:(lc + 1) * LANES]

    a_scr[...] = jnp.dot(u_ref[...], h2t_ref[...], preferred_element_type=F32)

    zero = jnp.zeros((BF16_ROWS, LANES), BF16)
    jtiles = N_KEYS // BF16_ROWS
    for ii in range(EB // N_KEYS):
        for lc in range(TB // LANES):
            lanes = slice(lc * LANES, (lc + 1) * LANES)
            g = [None] * jtiles
            for hh in range(PEER_HEADS):
                cnt = jnp.broadcast_to(cnt_ref[hh, ii:ii + 1, lanes], (BF16_ROWS, LANES)).astype(BF16)
                e0 = jnp.broadcast_to(e0_ref[hh, ii:ii + 1, lanes], (BF16_ROWS, LANES)).astype(BF16)
                for q in range(jtiles):
                    rk = rk_scr[hh, lc, q * BF16_ROWS:(q + 1) * BF16_ROWS, :]
                    e1 = e1_scr[hh, lc, q * BF16_ROWS:(q + 1) * BF16_ROWS, :]
                    term = jnp.minimum(jnp.maximum(cnt - rk, zero), e0) * e1
                    g[q] = term if g[q] is None else g[q] + term
            for q in range(jtiles):
                r0 = ii * N_KEYS + q * BF16_ROWS
                xb = a_scr[r0:r0 + BF16_ROWS, lanes].astype(BF16)
                th = jnp.tanh(xb * (GELU_C1 + GELU_C2 * (xb * xb)))
                z_scr[r0:r0 + BF16_ROWS, lanes] = (xb * th + xb) * g[q]
    acc_scr[...] += jnp.dot(vt_ref[0], z_scr[...], preferred_element_type=F32)

    @pl.when(j == pl.num_programs(1) - 1)
    def _():
        gt2 = cond_ref[0][5:6]
        x2 = x1_ref[...] + gt2 * acc_scr[...].T
        o_ref[...] = _rms(x2) * gf_ref[...]


def _peer_call(h2t, rk, e1, cnt, e0, u_bf, vt_bf, x1, cond3, gf, tiles_per_batch):
    T, D = x1.shape
    tok = lambda: pl.BlockSpec((PEER_HEADS, N_KEYS, TB), lambda i, j: (0, 0, i))
    rows = lambda: pl.BlockSpec((PEER_HEADS, EB // N_KEYS, TB), lambda i, j: (0, j, i))
    return pl.pallas_call(
        _peer_kernel,
        grid=(T // TB, N_EXPERTS // EB),
        in_specs=[pl.BlockSpec((D, TB), lambda i, j: (0, i)),
                  tok(), tok(), rows(), rows(),
                  pl.BlockSpec((EB, D), lambda i, j: (j, 0)),
                  pl.BlockSpec((1, D, EB), lambda i, j: (j, 0, 0)),
                  pl.BlockSpec((TB, D), lambda i, j: (i, 0)),
                  pl.BlockSpec((1, N_COND, D), lambda i, j: (i // tiles_per_batch, 0, 0)),
                  pl.BlockSpec((1, D), lambda i, j: (0, 0))],
        out_specs=pl.BlockSpec((TB, D), lambda i, j: (i, 0)),
        out_shape=jax.ShapeDtypeStruct((T, D), F32),
        scratch_shapes=[pltpu.VMEM((PEER_HEADS, TB // LANES, N_KEYS, LANES), BF16),
                        pltpu.VMEM((PEER_HEADS, TB // LANES, N_KEYS, LANES), BF16),
                        pltpu.VMEM((EB, TB), F32),
                        pltpu.VMEM((EB, TB), BF16),
                        pltpu.VMEM((D, TB), F32)],
        compiler_params=pltpu.CompilerParams(
            dimension_semantics=("arbitrary", "arbitrary"), vmem_limit_bytes=VMEM_LIMIT),
        name="peer",
    )(h2t, rk, e1, cnt, e0, u_bf, vt_bf, x1, cond3, gf)


def kernel(x, c, w_ada, b_ada, g_norm1, w_in, b_in, conv_w, conv_b, g_conv_ln, b_conv_ln,
           g_v_ln, b_v_ln, w_spatial, b_spatial, w_out, b_out, g_norm2, w_query, sub_keys,
           expert_u, expert_v, g_final):
    B, S, D = x.shape
    depth = w_ada.shape[0]
    assert depth == 1, "the final rmsnorm is fused into the PEER call of the only layer"
    T = B * S
    row = lambda v: v.reshape(1, -1)
    grp = jnp.arange(D_CONV) // GROUP
    mg = jnp.where(grp[:, None] == grp[None, :], 1.0 / GROUP, 0.0).astype(BF16)
    c_pad = jnp.pad(c, ((0, 8 - B), (0, 0)))
    for l in range(depth):
        cond = _cond_call(c_pad, w_ada[l], row(b_ada[l]))
        cond3 = cond[:B].reshape(B, N_COND, D)
        bs = jnp.repeat(b_spatial[l].T, GROUP, axis=1)
        x1 = _mixer_call(x, cond3, row(g_norm1[l]), w_in[l].astype(BF16), row(b_in[l]),
                         conv_w[l], row(conv_b[l]), row(g_conv_ln[l]), row(b_conv_ln[l]),
                         row(g_v_ln[l]), row(b_v_ln[l]), w_spatial[l], bs, mg,
                         w_out[l].astype(BF16), row(b_out[l]))
        x1 = x1.reshape(T, D)
        keys = sub_keys[l].reshape(2 * PEER_HEADS, N_KEYS, D_HALF).astype(BF16)
        h2t, rk, e1, cnt, e0 = _prep_call(x1, cond3, row(g_norm2[l]), w_query[l].astype(BF16),
                                          keys, S // TB)
        vt = jnp.swapaxes(expert_v[l].astype(BF16).reshape(N_EXPERTS // EB, EB, D), 1, 2)
        x = _peer_call(h2t, rk, e1, cnt, e0, expert_u[l].astype(BF16), vt, x1, cond3,
                       row(g_final), S // TB).reshape(B, S, D)
    return x
```

```python
import jax
import jax.numpy as jnp
from jax import lax
from jax.experimental import pallas as pl
from jax.experimental.pallas import tpu as pltpu

F32 = jnp.float32
BF16 = jnp.bfloat16

D_MODEL = 1024
D_CONV = 512
D_GMLP = 512
GROUP = 64
N_HEADS_G = 8
CONV_WIDTH = 31
CHUNK = 128
N_PROJ = 2048
N_KEYS = 128
N_EXPERTS = N_KEYS * N_KEYS
PEER_HEADS = 8
D_HALF = 128
TOPK = 16
N_COND = 6
EPS = 1e-6
GELU_C1 = 0.7978845608028654
GELU_C2 = 0.044715 * GELU_C1

V7X_VMEM_BYTES = 64 * 1024 * 1024
VMEM_LIMIT = V7X_VMEM_BYTES * 3 // 4
SUBLANES = 8
LANES = 128
BF16_ROWS = 16

TS = 512
CONV_HALO = 32
CONV_ROWS = 64
TB = 512
EB = 2048


def _rms(x):
    return x * lax.rsqrt(jnp.mean(x * x, axis=-1, keepdims=True) + EPS)


def _split_dot(x, m):
    hi = x.astype(BF16)
    lo = (x - hi.astype(F32)).astype(BF16)
    return (jnp.dot(hi, m, preferred_element_type=F32)
            + jnp.dot(lo, m, preferred_element_type=F32))


def _group_ln(y, mg, g, b):
    mu = _split_dot(y, mg)
    d = y - mu
    var = _split_dot(d * d, mg)
    return d * lax.rsqrt(var + EPS) * g + b


def _cond_kernel(c_ref, w_ref, b_ref, o_ref):
    c = c_ref[...]
    s = c * jax.nn.sigmoid(c)
    o_ref[...] = jnp.dot(s, w_ref[...], preferred_element_type=F32,
                         precision=lax.Precision.HIGHEST) + b_ref[...]


def _cond_call(c_pad, w_ada, b_ada):
    n = w_ada.shape[1]
    blk = 1024
    return pl.pallas_call(
        _cond_kernel,
        grid=(n // blk,),
        in_specs=[pl.BlockSpec((8, D_MODEL), lambda j: (0, 0)),
                  pl.BlockSpec((D_MODEL, blk), lambda j: (0, j)),
                  pl.BlockSpec((1, blk), lambda j: (0, j))],
        out_specs=pl.BlockSpec((8, blk), lambda j: (0, j)),
        out_shape=jax.ShapeDtypeStruct((8, n), F32),
        compiler_params=pltpu.CompilerParams(
            dimension_semantics=("arbitrary",), vmem_limit_bytes=VMEM_LIMIT),
        name="cond",
    )(c_pad, w_ada, b_ada)


def _mixer_kernel(x_ref, cond_ref, g1_ref, win_ref, bin_ref, cw_ref, cb_ref,
                  gcl_ref, bcl_ref, gvl_ref, bvl_ref, ws_ref, bs_ref, mg_ref,
                  wout_ref, bout_ref, x1_ref, abuf, ybuf):
    s = pl.program_id(1)
    x = x_ref[0]
    cond = cond_ref[0]
    sh1, sc1, gt1 = cond[0:1], cond[1:2], cond[2:3]
    h = _rms(x) * g1_ref[...] * (1.0 + sc1) + sh1
    p = jnp.dot(h.astype(BF16), win_ref[...], preferred_element_type=F32) + bin_ref[...]

    a = p[:, :D_CONV] * jax.nn.sigmoid(p[:, D_CONV:2 * D_CONV])

    @pl.when(s == 0)
    def _():
        abuf[0:CONV_HALO, :] = jnp.zeros((CONV_HALO, D_CONV), F32)

    @pl.when(s > 0)
    def _():
        abuf[0:CONV_HALO, :] = abuf[TS:TS + CONV_HALO, :]

    abuf[CONV_HALO:CONV_HALO + TS, :] = a

    off = CONV_HALO - (CONV_WIDTH - 1)
    win_rows = CONV_ROWS + CONV_HALO
    for r in range(TS // CONV_ROWS):
        base = r * CONV_ROWS
        window = abuf[base:base + win_rows, :]
        shifted = {0: window}
        acc = jnp.zeros((CONV_ROWS, D_CONV), F32) + cb_ref[...]
        for k in range(CONV_WIDTH):
            res = (off + k) % SUBLANES
            lo = off + k - res
            if res not in shifted:
                shifted[res] = pltpu.roll(window, win_rows - res, axis=0)
            acc = acc + shifted[res][lo:lo + CONV_ROWS, :] * cw_ref[k:k + 1, :]
        ybuf[base:base + CONV_ROWS, :] = acc
    mg = mg_ref[...]
    ya = _group_ln(ybuf[...], mg, gcl_ref[...], bcl_ref[...])
    ya = ya * jax.nn.sigmoid(ya)

    z = jax.nn.gelu(p[:, 2 * D_CONV:])
    u = z[:, :D_GMLP]
    v = _group_ln(z[:, D_GMLP:], mg, gvl_ref[...], bvl_ref[...])
    row = lax.broadcasted_iota(jnp.int32, (CHUNK, CHUNK), 0)
    col = lax.broadcasted_iota(jnp.int32, (CHUNK, CHUNK), 1)
    lane_head = lax.broadcasted_iota(jnp.int32, (CHUNK, D_GMLP), 1) // GROUP
    w_heads = [jnp.where(row >= col, ws_ref[hh], 0.0).astype(BF16) for hh in range(N_HEADS_G)]
    yb_chunks = []
    for ch in range(TS // CHUNK):
        vc = v[ch * CHUNK:(ch + 1) * CHUNK, :]
        mixed = bs_ref[...]
        for hh in range(N_HEADS_G):
            vm = jnp.where(lane_head == hh, vc, 0.0).astype(BF16)
            mixed = mixed + jnp.dot(w_heads[hh], vm, preferred_element_type=F32)
        yb_chunks.append(u[ch * CHUNK:(ch + 1) * CHUNK, :] * mixed)
    yb = jnp.concatenate(yb_chunks, axis=0)

    y = (jnp.dot(ya.astype(BF16), wout_ref[0:D_CONV, :], preferred_element_type=F32)
         + jnp.dot(yb.astype(BF16), wout_ref[D_CONV:, :], preferred_element_type=F32)
         + bout_ref[...])
    x1_ref[0] = x + gt1 * y


def _mixer_call(x, cond3, g1, win, bin_, cw, cb, gcl, bcl, gvl, bvl, ws, bs, mg, wout, bout):
    B, S, D = x.shape
    full = lambda shape: pl.BlockSpec(shape, lambda b, s: (0,) * len(shape))
    return pl.pallas_call(
        _mixer_kernel,
        grid=(B, S // TS),
        in_specs=[pl.BlockSpec((1, TS, D), lambda b, s: (b, s, 0)),
                  pl.BlockSpec((1, N_COND, D), lambda b, s: (b, 0, 0)),
                  full((1, D)), full((D, N_PROJ)), full((1, N_PROJ)),
                  full((CONV_WIDTH, D_CONV)), full((1, D_CONV)),
                  full((1, D_CONV)), full((1, D_CONV)), full((1, D_GMLP)), full((1, D_GMLP)),
                  full((N_HEADS_G, CHUNK, CHUNK)), full((CHUNK, D_GMLP)),
                  full((D_CONV, D_CONV)), full((D, D)), full((1, D))],
        out_specs=pl.BlockSpec((1, TS, D), lambda b, s: (b, s, 0)),
        out_shape=jax.ShapeDtypeStruct((B, S, D), F32),
        scratch_shapes=[pltpu.VMEM((TS + CONV_HALO, D_CONV), F32),
                        pltpu.VMEM((TS, D_CONV), F32)],
        compiler_params=pltpu.CompilerParams(
            dimension_semantics=("arbitrary", "arbitrary"), vmem_limit_bytes=VMEM_LIMIT),
        name="mixer",
    )(x, cond3, g1, win, bin_, cw, cb, gcl, bcl, gvl, bvl, ws, bs, mg, wout, bout)


def _oddeven_merge(lo, hi, r):
    step = r * 2
    if step < hi - lo:
        yield from _oddeven_merge(lo, hi, step)
        yield from _oddeven_merge(lo + r, hi, step)
        yield from [(i, i + r) for i in range(lo + r, hi - r, step)]
    else:
        yield (lo, lo + r)


def _oddeven_merge_sort(lo, hi):
    if hi - lo >= 1:
        mid = lo + (hi - lo) // 2
        yield from _oddeven_merge_sort(lo, mid)
        yield from _oddeven_merge_sort(mid + 1, hi)
        yield from _oddeven_merge(lo, hi, 1)


SORT16 = tuple(_oddeven_merge_sort(0, TOPK - 1))
BITONIC16 = tuple((k, k + d) for d in (8, 4, 2, 1) for k in range(TOPK) if not k & d)


def _exchange(v, pairs):
    for i, j in pairs:
        v[i], v[j] = jnp.maximum(v[i], v[j]), jnp.minimum(v[i], v[j])
    return v


def _top16_sorted(s):
    v = _exchange([s[SUBLANES * k:SUBLANES * (k + 1), :] for k in range(TOPK)], SORT16)
    for shift in (4, 2, 1):
        v = _exchange([jnp.maximum(v[k], pltpu.roll(v[TOPK - 1 - k], shift, axis=0))
                       for k in range(TOPK)], BITONIC16)
    return v


def _top_values(work, n):
    vals = []
    for _ in range(n):
        m = jnp.max(work, axis=0, keepdims=True)
        vals.append(m)
        work = jnp.where(work == m, -jnp.inf, work)
    return vals


def _rows_of(vals, sublane):
    out = vals[SUBLANES - 1]
    for r in range(SUBLANES - 2, -1, -1):
        out = jnp.where(sublane == r, vals[r], out)
    return out


def _prep_kernel(x1_ref, cond_ref, g2_ref, wq_ref, keys_ref,
                 h2t_ref, rk_ref, e1_ref, cnt_ref, e0_ref, q_scr, s_scr):
    cond = cond_ref[0]
    sh2, sc2 = cond[3:4], cond[4:5]
    h2 = _rms(x1_ref[...]) * g2_ref[...] * (1.0 + sc2) + sh2
    h2t_ref[...] = h2.T.astype(BF16)
    q_scr[...] = jnp.dot(h2.astype(BF16), wq_ref[...], preferred_element_type=F32)

    def head(hh, carry):
        c0 = pl.multiple_of(hh * (2 * D_HALF), 2 * D_HALF)
        q0 = q_scr[:, pl.ds(c0, D_HALF)].astype(BF16)
        q1 = q_scr[:, pl.ds(c0 + D_HALF, D_HALF)].astype(BF16)
        nt = (((1,), (1,)), ((), ()))
        s_scr[0] = lax.dot_general(keys_ref[2 * hh], q0, nt, preferred_element_type=F32)
        s_scr[1] = lax.dot_general(keys_ref[2 * hh + 1], q1, nt, preferred_element_type=F32)

        def chunk(lc, carry2):
            lanes = pl.ds(pl.multiple_of(lc * LANES, LANES), LANES)
            s0 = s_scr[0, :, lanes]
            s1 = s_scr[1, :, lanes]
            a = _top16_sorted(s0)
            b = _top16_sorted(s1)
            sublane = lax.broadcasted_iota(jnp.int32, (SUBLANES, LANES), 0)
            a_lo, a_hi = _rows_of(a[:SUBLANES], sublane), _rows_of(a[SUBLANES:], sublane)
            b_hi = _rows_of(b[SUBLANES:], sublane)
            cand = [a_lo + b[0], a_hi + b[0]] + [a_lo + b[c] for c in range(1, SUBLANES)]
            cand.append(a[0] + b_hi)
            top = _top_values(jnp.concatenate(cand, axis=0), TOPK)
            tau = top[TOPK - 1]
            zsum = jnp.zeros_like(tau)
            for t in top:
                zsum = zsum + jnp.exp(t - top[0])
            rank1 = jnp.zeros(s1.shape, F32)
            count = jnp.zeros(s0.shape, F32)
            for c in range(TOPK):
                bc = jnp.concatenate([b[c]] * (N_KEYS // SUBLANES), axis=0)
                rank1 = jnp.where(bc > s1, float(c + 1), rank1)
                count = jnp.where(s0 + bc >= tau, float(c + 1), count)
            a0 = jnp.concatenate([a[0]] * (N_KEYS // SUBLANES), axis=0)
            b0 = jnp.concatenate([b[0]] * (N_KEYS // SUBLANES), axis=0)
            rk_ref[hh, :, lanes] = rank1.astype(BF16)
            e1_ref[hh, :, lanes] = (jnp.exp(s1 - b0) * (0.5 / zsum)).astype(BF16)
            cnt_ref[hh, :, lanes] = count
            e0_ref[hh, :, lanes] = jnp.exp(s0 - a0)
            return carry2

        lax.fori_loop(0, TB // LANES, chunk, 0, unroll=True)
        return carry

    lax.fori_loop(0, PEER_HEADS, head, 0)


def _prep_call(x1, cond3, g2, wq, keys, tiles_per_batch):
    T, D = x1.shape
    tok = lambda: pl.BlockSpec((PEER_HEADS, N_KEYS, TB), lambda i: (0, 0, i))
    sds = lambda dt: jax.ShapeDtypeStruct((PEER_HEADS, N_KEYS, T), dt)
    return pl.pallas_call(
        _prep_kernel,
        grid=(T // TB,),
        in_specs=[pl.BlockSpec((TB, D), lambda i: (i, 0)),
                  pl.BlockSpec((1, N_COND, D), lambda i: (i // tiles_per_batch, 0, 0)),
                  pl.BlockSpec((1, D), lambda i: (0, 0)),
                  pl.BlockSpec((D, 2 * PEER_HEADS * D_HALF), lambda i: (0, 0)),
                  pl.BlockSpec((2 * PEER_HEADS, N_KEYS, D_HALF), lambda i: (0, 0, 0))],
        out_specs=[pl.BlockSpec((D, TB), lambda i: (0, i)), tok(), tok(), tok(), tok()],
        out_shape=[jax.ShapeDtypeStruct((D, T), BF16),
                   sds(BF16), sds(BF16), sds(F32), sds(F32)],
        scratch_shapes=[pltpu.VMEM((TB, 2 * PEER_HEADS * D_HALF), F32),
                        pltpu.VMEM((2, N_KEYS, TB), F32)],
        compiler_params=pltpu.CompilerParams(
            dimension_semantics=("arbitrary",), vmem_limit_bytes=VMEM_LIMIT),
        name="prep",
    )(x1, cond3, g2, wq, keys)


def _peer_kernel(h2t_ref, rk_ref, e1_ref, cnt_ref, e0_ref, u_ref, vt_ref,
                 x1_ref, cond_ref, gf_ref, o_ref, rk_scr, e1_scr, a_scr, z_scr, acc_scr):
    j = pl.program_id(1)

    @pl.when(j == 0)
    def _():
        acc_scr[...] = jnp.zeros_like(acc_scr)
        for lc in range(TB // LANES):
            rk_scr[:, lc] = rk_ref[:, :, lc * LANES:(lc + 1) * LANES]
            e1_scr[:, lc] = e1_ref[:, :, lc * LANES:(lc + 1) * LANES]

    a_scr[...] = jnp.dot(u_ref[...], h2t_ref[...], preferred_element_type=F32)

    zero = jnp.zeros((BF16_ROWS, LANES), BF16)
    jtiles = N_KEYS // BF16_ROWS
    for ii in range(EB // N_KEYS):
        for lc in range(TB // LANES):
            lanes = slice(lc * LANES, (lc + 1) * LANES)
            g = [None] * jtiles
            for hh in range(PEER_HEADS):
                cnt = jnp.broadcast_to(cnt_ref[hh, ii:ii + 1, lanes], (BF16_ROWS, LANES)).astype(BF16)
                e0 = jnp.broadcast_to(e0_ref[hh, ii:ii + 1, lanes], (BF16_ROWS, LANES)).astype(BF16)
                for q in range(jtiles):
                    rk = rk_scr[hh, lc, q * BF16_ROWS:(q + 1) * BF16_ROWS, :]
                    e1 = e1_scr[hh, lc, q * BF16_ROWS:(q + 1) * BF16_ROWS, :]
                    term = jnp.minimum(jnp.maximum(cnt - rk, zero), e0) * e1
                    g[q] = term if g[q] is None else g[q] + term
            for q in range(jtiles):
                r0 = ii * N_KEYS + q * BF16_ROWS
                xb = a_scr[r0:r0 + BF16_ROWS, lanes].astype(BF16)
                th = jnp.tanh(xb * (GELU_C1 + GELU_C2 * (xb * xb)))
                z_scr[r0:r0 + BF16_ROWS, lanes] = (xb * th + xb) * g[q]
    acc_scr[...] += jnp.dot(vt_ref[0], z_scr[...], preferred_element_type=F32)

    @pl.when(j == pl.num_programs(1) - 1)
    def _():
        gt2 = cond_ref[0][5:6]
        x2 = x1_ref[...] + gt2 * acc_scr[...].T
        o_ref[...] = _rms(x2) * gf_ref[...]


def _peer_call(h2t, rk, e1, cnt, e0, u_bf, vt_bf, x1, cond3, gf, tiles_per_batch):
    T, D = x1.shape
    tok = lambda: pl.BlockSpec((PEER_HEADS, N_KEYS, TB), lambda i, j: (0, 0, i))
    rows = lambda: pl.BlockSpec((PEER_HEADS, EB // N_KEYS, TB), lambda i, j: (0, j, i))
    return pl.pallas_call(
        _peer_kernel,
        grid=(T // TB, N_EXPERTS // EB),
        in_specs=[pl.BlockSpec((D, TB), lambda i, j: (0, i)),
                  tok(), tok(), rows(), rows(),
                  pl.BlockSpec((EB, D), lambda i, j: (j, 0)),
                  pl.BlockSpec((1, D, EB), lambda i, j: (j, 0, 0)),
                  pl.BlockSpec((TB, D), lambda i, j: (i, 0)),
                  pl.BlockSpec((1, N_COND, D), lambda i, j: (i // tiles_per_batch, 0, 0)),
                  pl.BlockSpec((1, D), lambda i, j: (0, 0))],
        out_specs=pl.BlockSpec((TB, D), lambda i, j: (i, 0)),
        out_shape=jax.ShapeDtypeStruct((T, D), F32),
        scratch_shapes=[pltpu.VMEM((PEER_HEADS, TB // LANES, N_KEYS, LANES), BF16),
                        pltpu.VMEM((PEER_HEADS, TB // LANES, N_KEYS, LANES), BF16),
                        pltpu.VMEM((EB, TB), F32),
                        pltpu.VMEM((EB, TB), BF16),
                        pltpu.VMEM((D, TB), F32)],
        compiler_params=pltpu.CompilerParams(
            dimension_semantics=("arbitrary", "arbitrary"), vmem_limit_bytes=VMEM_LIMIT),
        name="peer",
    )(h2t, rk, e1, cnt, e0, u_bf, vt_bf, x1, cond3, gf)


def kernel(x, c, w_ada, b_ada, g_norm1, w_in, b_in, conv_w, conv_b, g_conv_ln, b_conv_ln,
           g_v_ln, b_v_ln, w_spatial, b_spatial, w_out, b_out, g_norm2, w_query, sub_keys,
           expert_u, expert_v, g_final):
    B, S, D = x.shape
    depth = w_ada.shape[0]
    assert depth == 1, "the final rmsnorm is fused into the PEER call of the only layer"
    T = B * S
    row = lambda v: v.reshape(1, -1)
    grp = jnp.arange(D_CONV) // GROUP
    mg = jnp.where(grp[:, None] == grp[None, :], 1.0 / GROUP, 0.0).astype(BF16)
    c_pad = jnp.pad(c, ((0, 8 - B), (0, 0)))
    for l in range(depth):
        cond = _cond_call(c_pad, w_ada[l], row(b_ada[l]))
        cond3 = cond[:B].reshape(B, N_COND, D)
        bs = jnp.repeat(b_spatial[l].T, GROUP, axis=1)
        x1 = _mixer_call(x, cond3, row(g_norm1[l]), w_in[l].astype(BF16), row(b_in[l]),
                         conv_w[l], row(conv_b[l]), row(g_conv_ln[l]), row(b_conv_ln[l]),
                         row(g_v_ln[l]), row(b_v_ln[l]), w_spatial[l], bs, mg,
                         w_out[l].astype(BF16), row(b_out[l]))
        x1 = x1.reshape(T, D)
        keys = sub_keys[l].reshape(2 * PEER_HEADS, N_KEYS, D_HALF).astype(BF16)
        h2t, rk, e1, cnt, e0 = _prep_call(x1, cond3, row(g_norm2[l]), w_query[l].astype(BF16),
                                          keys, S // TB)
        vt = jnp.swapaxes(expert_v[l].astype(BF16).reshape(N_EXPERTS // EB, EB, D), 1, 2)
        x = _peer_call(h2t, rk, e1, cnt, e0, expert_u[l].astype(BF16), vt, x1, cond3,
                       row(g_final), S // TB).reshape(B, S, D)
    return x
```

```python
import jax
import jax.numpy as jnp
from jax import lax
from jax.experimental import pallas as pl
from jax.experimental.pallas import tpu as pltpu

F32 = jnp.float32
BF16 = jnp.bfloat16

D_MODEL = 1024
D_CONV = 512
D_GMLP = 512
GROUP = 64
N_HEADS_G = 8
CONV_WIDTH = 31
CHUNK = 128
N_PROJ = 2048
N_KEYS = 128
N_EXPERTS = N_KEYS * N_KEYS
PEER_HEADS = 8
D_HALF = 128
TOPK = 16
N_COND = 6
EPS = 1e-6
GELU_C1 = 0.7978845608028654
GELU_C2 = 0.044715 * GELU_C1

V7X_VMEM_BYTES = 64 * 1024 * 1024
VMEM_LIMIT = V7X_VMEM_BYTES * 3 // 4
SUBLANES = 8
LANES = 128
BF16_ROWS = 16

TS = 512
CONV_HALO = 32
CONV_ROWS = 64
TB = 512
EB = 2048


def _rms(x):
    return x * lax.rsqrt(jnp.mean(x * x, axis=-1, keepdims=True) + EPS)


def _split_dot(x, m):
    hi = x.astype(BF16)
    lo = (x - hi.astype(F32)).astype(BF16)
    return (jnp.dot(hi, m, preferred_element_type=F32)
            + jnp.dot(lo, m, preferred_element_type=F32))


def _group_ln(y, mg, g, b):
    mu = _split_dot(y, mg)
    d = y - mu
    var = _split_dot(d * d, mg)
    return d * lax.rsqrt(var + EPS) * g + b


def _cond_kernel(c_ref, w_ref, b_ref, o_ref):
    c = c_ref[...]
    s = c * jax.nn.sigmoid(c)
    o_ref[...] = jnp.dot(s, w_ref[...], preferred_element_type=F32,
                         precision=lax.Precision.HIGHEST) + b_ref[...]


def _cond_call(c_pad, w_ada, b_ada):
    n = w_ada.shape[1]
    blk = 1024
    return pl.pallas_call(
        _cond_kernel,
        grid=(n // blk,),
        in_specs=[pl.BlockSpec((8, D_MODEL), lambda j: (0, 0)),
                  pl.BlockSpec((D_MODEL, blk), lambda j: (0, j)),
                  pl.BlockSpec((1, blk), lambda j: (0, j))],
        out_specs=pl.BlockSpec((8, blk), lambda j: (0, j)),
        out_shape=jax.ShapeDtypeStruct((8, n), F32),
        compiler_params=pltpu.CompilerParams(
            dimension_semantics=("arbitrary",), vmem_limit_bytes=VMEM_LIMIT),
        name="cond",
    )(c_pad, w_ada, b_ada)


def _mixer_kernel(x_ref, cond_ref, g1_ref, win_ref, bin_ref, cw_ref, cb_ref,
                  gcl_ref, bcl_ref, gvl_ref, bvl_ref, ws_ref, bs_ref, mg_ref,
                  wout_ref, bout_ref, x1_ref, abuf, ybuf):
    s = pl.program_id(1)
    x = x_ref[0]
    cond = cond_ref[0]
    sh1, sc1, gt1 = cond[0:1], cond[1:2], cond[2:3]
    h = _rms(x) * g1_ref[...] * (1.0 + sc1) + sh1
    p = jnp.dot(h.astype(BF16), win_ref[...], preferred_element_type=F32) + bin_ref[...]

    a = p[:, :D_CONV] * jax.nn.sigmoid(p[:, D_CONV:2 * D_CONV])

    @pl.when(s == 0)
    def _():
        abuf[0:CONV_HALO, :] = jnp.zeros((CONV_HALO, D_CONV), F32)

    @pl.when(s > 0)
    def _():
        abuf[0:CONV_HALO, :] = abuf[TS:TS + CONV_HALO, :]

    abuf[CONV_HALO:CONV_HALO + TS, :] = a

    off = CONV_HALO - (CONV_WIDTH - 1)
    win_rows = CONV_ROWS + CONV_HALO
    for r in range(TS // CONV_ROWS):
        base = r * CONV_ROWS
        window = abuf[base:base + win_rows, :]
        shifted = {0: window}
        acc = jnp.zeros((CONV_ROWS, D_CONV), F32) + cb_ref[...]
        for k in range(CONV_WIDTH):
            res = (off + k) % SUBLANES
            lo = off + k - res
            if res not in shifted:
                shifted[res] = pltpu.roll(window, win_rows - res, axis=0)
            acc = acc + shifted[res][lo:lo + CONV_ROWS, :] * cw_ref[k:k + 1, :]
        ybuf[base:base + CONV_ROWS, :] = acc
    mg = mg_ref[...]
    ya = _group_ln(ybuf[...], mg, gcl_ref[...], bcl_ref[...])
    ya = ya * jax.nn.sigmoid(ya)

    z = jax.nn.gelu(p[:, 2 * D_CONV:])
    u = z[:, :D_GMLP]
    v = _group_ln(z[:, D_GMLP:], mg, gvl_ref[...], bvl_ref[...])
    row = lax.broadcasted_iota(jnp.int32, (CHUNK, CHUNK), 0)
    col = lax.broadcasted_iota(jnp.int32, (CHUNK, CHUNK), 1)
    lane_head = lax.broadcasted_iota(jnp.int32, (CHUNK, D_GMLP), 1) // GROUP
    w_heads = [jnp.where(row >= col, ws_ref[hh], 0.0).astype(BF16) for hh in range(N_HEADS_G)]
    yb_chunks = []
    for ch in range(TS // CHUNK):
        vc = v[ch * CHUNK:(ch + 1) * CHUNK, :]
        mixed = bs_ref[...]
        for hh in range(N_HEADS_G):
            vm = jnp.where(lane_head == hh, vc, 0.0).astype(BF16)
            mixed = mixed + jnp.dot(w_heads[hh], vm, preferred_element_type=F32)
        yb_chunks.append(u[ch * CHUNK:(ch + 1) * CHUNK, :] * mixed)
    yb = jnp.concatenate(yb_chunks, axis=0)

    y = (jnp.dot(ya.astype(BF16), wout_ref[0:D_CONV, :], preferred_element_type=F32)
         + jnp.dot(yb.astype(BF16), wout_ref[D_CONV:, :], preferred_element_type=F32)
         + bout_ref[...])
    x1_ref[0] = x + gt1 * y


def _mixer_call(x, cond3, g1, win, bin_, cw, cb, gcl, bcl, gvl, bvl, ws, bs, mg, wout, bout):
    B, S, D = x.shape
    full = lambda shape: pl.BlockSpec(shape, lambda b, s: (0,) * len(shape))
    return pl.pallas_call(
        _mixer_kernel,
        grid=(B, S // TS),
        in_specs=[pl.BlockSpec((1, TS, D), lambda b, s: (b, s, 0)),
                  pl.BlockSpec((1, N_COND, D), lambda b, s: (b, 0, 0)),
                  full((1, D)), full((D, N_PROJ)), full((1, N_PROJ)),
                  full((CONV_WIDTH, D_CONV)), full((1, D_CONV)),
                  full((1, D_CONV)), full((1, D_CONV)), full((1, D_GMLP)), full((1, D_GMLP)),
                  full((N_HEADS_G, CHUNK, CHUNK)), full((CHUNK, D_GMLP)),
                  full((D_CONV, D_CONV)), full((D, D)), full((1, D))],
        out_specs=pl.BlockSpec((1, TS, D), lambda b, s: (b, s, 0)),
        out_shape=jax.ShapeDtypeStruct((B, S, D), F32),
        scratch_shapes=[pltpu.VMEM((TS + CONV_HALO, D_CONV), F32),
                        pltpu.VMEM((TS, D_CONV), F32)],
        compiler_params=pltpu.CompilerParams(
            dimension_semantics=("arbitrary", "arbitrary"), vmem_limit_bytes=VMEM_LIMIT),
        name="mixer",
    )(x, cond3, g1, win, bin_, cw, cb, gcl, bcl, gvl, bvl, ws, bs, mg, wout, bout)


def _oddeven_merge(lo, hi, r):
    step = r * 2
    if step < hi - lo:
        yield from _oddeven_merge(lo, hi, step)
        yield from _oddeven_merge(lo + r, hi, step)
        yield from [(i, i + r) for i in range(lo + r, hi - r, step)]
    else:
        yield (lo, lo + r)


def _oddeven_merge_sort(lo, hi):
    if hi - lo >= 1:
        mid = lo + (hi - lo) // 2
        yield from _oddeven_merge_sort(lo, mid)
        yield from _oddeven_merge_sort(mid + 1, hi)
        yield from _oddeven_merge(lo, hi, 1)


SORT16 = tuple(_oddeven_merge_sort(0, TOPK - 1))
BITONIC16 = tuple((k, k + d) for d in (8, 4, 2, 1) for k in range(TOPK) if not k & d)


def _exchange(v, pairs):
    for i, j in pairs:
        v[i], v[j] = jnp.maximum(v[i], v[j]), jnp.minimum(v[i], v[j])
    return v


def _top16_sorted(s):
    v = _exchange([s[SUBLANES * k:SUBLANES * (k + 1), :] for k in range(TOPK)], SORT16)
    for shift in (4, 2, 1):
        v = _exchange([jnp.maximum(v[k], pltpu.roll(v[TOPK - 1 - k], shift, axis=0))
                       for k in range(TOPK)], BITONIC16)
    return v


def _top_values(work, n):
    vals = []
    for _ in range(n):
        m = jnp.max(work, axis=0, keepdims=True)
        vals.append(m)
        work = jnp.where(work == m, -jnp.inf, work)
    return vals


def _rows_of(vals, sublane):
    out = vals[SUBLANES - 1]
    for r in range(SUBLANES - 2, -1, -1):
        out = jnp.where(sublane == r, vals[r], out)
    return out


def _prep_kernel(x1_ref, cond_ref, g2_ref, wq_ref, keys_ref,
                 h2t_ref, rk_ref, e1_ref, cnt_ref, e0_ref, q_scr, s_scr):
    cond = cond_ref[0]
    sh2, sc2 = cond[3:4], cond[4:5]
    h2 = _rms(x1_ref[...]) * g2_ref[...] * (1.0 + sc2) + sh2
    h2t_ref[...] = h2.T.astype(BF16)
    q_scr[...] = jnp.dot(h2.astype(BF16), wq_ref[...], preferred_element_type=F32)

    def head(hh, carry):
        c0 = pl.multiple_of(hh * (2 * D_HALF), 2 * D_HALF)
        q0 = q_scr[:, pl.ds(c0, D_HALF)].astype(BF16)
        q1 = q_scr[:, pl.ds(c0 + D_HALF, D_HALF)].astype(BF16)
        nt = (((1,), (1,)), ((), ()))
        s_scr[0] = lax.dot_general(keys_ref[2 * hh], q0, nt, preferred_element_type=F32)
        s_scr[1] = lax.dot_general(keys_ref[2 * hh + 1], q1, nt, preferred_element_type=F32)

        def chunk(lc, carry2):
            lanes = pl.ds(pl.multiple_of(lc * LANES, LANES), LANES)
            s0 = s_scr[0, :, lanes]
            s1 = s_scr[1, :, lanes]
            a = _top16_sorted(s0)
            b = _top16_sorted(s1)
            sublane = lax.broadcasted_iota(jnp.int32, (SUBLANES, LANES), 0)
            a_lo, a_hi = _rows_of(a[:SUBLANES], sublane), _rows_of(a[SUBLANES:], sublane)
            b_hi = _rows_of(b[SUBLANES:], sublane)
            cand = [a_lo + b[0], a_hi + b[0]] + [a_lo + b[c] for c in range(1, SUBLANES)]
            cand.append(a[0] + b_hi)
            top = _top_values(jnp.concatenate(cand, axis=0), TOPK)
            tau = top[TOPK - 1]
            zsum = jnp.zeros_like(tau)
            for t in top:
                zsum = zsum + jnp.exp(t - top[0])
            rank1 = jnp.zeros(s1.shape, F32)
            count = jnp.zeros(s0.shape, F32)
            for c in range(TOPK):
                bc = jnp.concatenate([b[c]] * (N_KEYS // SUBLANES), axis=0)
                rank1 = jnp.where(bc > s1, float(c + 1), rank1)
                count = jnp.where(s0 + bc >= tau, float(c + 1), count)
            a0 = jnp.concatenate([a[0]] * (N_KEYS // SUBLANES), axis=0)
            b0 = jnp.concatenate([b[0]] * (N_KEYS // SUBLANES), axis=0)
            rk_ref[hh, :, lanes] = rank1.astype(BF16)
            e1_ref[hh, :, lanes] = (jnp.exp(s1 - b0) * (0.5 / zsum)).astype(BF16)
            cnt_ref[hh, :, lanes] = count
            e0_ref[hh, :, lanes] = jnp.exp(s0 - a0)
            return carry2

        lax.fori_loop(0, TB // LANES, chunk, 0, unroll=True)
        return carry

    lax.fori_loop(0, PEER_HEADS, head, 0, unroll=2)


def _prep_call(x1, cond3, g2, wq, keys, tiles_per_batch):
    T, D = x1.shape
    tok = lambda: pl.BlockSpec((PEER_HEADS, N_KEYS, TB), lambda i: (0, 0, i))
    sds = lambda dt: jax.ShapeDtypeStruct((PEER_HEADS, N_KEYS, T), dt)
    return pl.pallas_call(
        _prep_kernel,
        grid=(T // TB,),
        in_specs=[pl.BlockSpec((TB, D), lambda i: (i, 0)),
                  pl.BlockSpec((1, N_COND, D), lambda i: (i // tiles_per_batch, 0, 0)),
                  pl.BlockSpec((1, D), lambda i: (0, 0)),
                  pl.BlockSpec((D, 2 * PEER_HEADS * D_HALF), lambda i: (0, 0)),
                  pl.BlockSpec((2 * PEER_HEADS, N_KEYS, D_HALF), lambda i: (0, 0, 0))],
        out_specs=[pl.BlockSpec((D, TB), lambda i: (0, i)), tok(), tok(), tok(), tok()],
        out_shape=[jax.ShapeDtypeStruct((D, T), BF16),
                   sds(BF16), sds(BF16), sds(F32), sds(F32)],
        scratch_shapes=[pltpu.VMEM((TB, 2 * PEER_HEADS * D_HALF), F32),
                        pltpu.VMEM((2, N_KEYS, TB), F32)],
        compiler_params=pltpu.CompilerParams(
            dimension_semantics=("arbitrary",), vmem_limit_bytes=VMEM_LIMIT),
        name="prep",
    )(x1, cond3, g2, wq, keys)


def _peer_kernel(h2t_ref, rk_ref, e1_ref, cnt_ref, e0_ref, u_ref, vt_ref,
                 x1_ref, cond_ref, gf_ref, o_ref, rk_scr, e1_scr, a_scr, z_scr, acc_scr):
    j = pl.program_id(1)

    @pl.when(j == 0)
    def _():
        acc_scr[...] = jnp.zeros_like(acc_scr)
        for lc in range(TB // LANES):
            rk_scr[:, lc] = rk_ref[:, :, lc * LANES:(lc + 1) * LANES]
            e1_scr[:, lc] = e1_ref[:, :, lc * LANES:(lc + 1) * LANES]

    a_scr[...] = jnp.dot(u_ref[...], h2t_ref[...], preferred_element_type=F32)

    zero = jnp.zeros((BF16_ROWS, LANES), BF16)
    jtiles = N_KEYS // BF16_ROWS
    for ii in range(EB // N_KEYS):
        for lc in range(TB // LANES):
            lanes = slice(lc * LANES, (lc + 1) * LANES)
            g = [None] * jtiles
            for hh in range(PEER_HEADS):
                cnt = jnp.broadcast_to(cnt_ref[hh, ii:ii + 1, lanes], (BF16_ROWS, LANES)).astype(BF16)
                e0 = jnp.broadcast_to(e0_ref[hh, ii:ii + 1, lanes], (BF16_ROWS, LANES)).astype(BF16)
                for q in range(jtiles):
                    rk = rk_scr[hh, lc, q * BF16_ROWS:(q + 1) * BF16_ROWS, :]
                    e1 = e1_scr[hh, lc, q * BF16_ROWS:(q + 1) * BF16_ROWS, :]
                    term = jnp.minimum(jnp.maximum(cnt - rk, zero), e0) * e1
                    g[q] = term if g[q] is None else g[q] + term
            for q in range(jtiles):
                r0 = ii * N_KEYS + q * BF16_ROWS
                xb = a_scr[r0:r0 + BF16_ROWS, lanes].astype(BF16)
                th = jnp.tanh(xb * (GELU_C1 + GELU_C2 * (xb * xb)))
                z_scr[r0:r0 + BF16_ROWS, lanes] = (xb * th + xb) * g[q]
    acc_scr[...] += jnp.dot(vt_ref[0], z_scr[...], preferred_element_type=F32)

    @pl.when(j == pl.num_programs(1) - 1)
    def _():
        gt2 = cond_ref[0][5:6]
        x2 = x1_ref[...] + gt2 * acc_scr[...].T
        o_ref[...] = _rms(x2) * gf_ref[...]


def _peer_call(h2t, rk, e1, cnt, e0, u_bf, vt_bf, x1, cond3, gf, tiles_per_batch):
    T, D = x1.shape
    tok = lambda: pl.BlockSpec((PEER_HEADS, N_KEYS, TB), lambda i, j: (0, 0, i))
    rows = lambda: pl.BlockSpec((PEER_HEADS, EB // N_KEYS, TB), lambda i, j: (0, j, i))
    return pl.pallas_call(
        _peer_kernel,
        grid=(T // TB, N_EXPERTS // EB),
        in_specs=[pl.BlockSpec((D, TB), lambda i, j: (0, i)),
                  tok(), tok(), rows(), rows(),
                  pl.BlockSpec((EB, D), lambda i, j: (j, 0)),
                  pl.BlockSpec((1, D, EB), lambda i, j: (j, 0, 0)),
                  pl.BlockSpec((TB, D), lambda i, j: (i, 0)),
                  pl.BlockSpec((1, N_COND, D), lambda i, j: (i // tiles_per_batch, 0, 0)),
                  pl.BlockSpec((1, D), lambda i, j: (0, 0))],
        out_specs=pl.BlockSpec((TB, D), lambda i, j: (i, 0)),
        out_shape=jax.ShapeDtypeStruct((T, D), F32),
        scratch_shapes=[pltpu.VMEM((PEER_HEADS, TB // LANES, N_KEYS, LANES), BF16),
                        pltpu.VMEM((PEER_HEADS, TB // LANES, N_KEYS, LANES), BF16),
                        pltpu.VMEM((EB, TB), F32),
                        pltpu.VMEM((EB, TB), BF16),
                        pltpu.VMEM((D, TB), F32)],
        compiler_params=pltpu.CompilerParams(
            dimension_semantics=("arbitrary", "arbitrary"), vmem_limit_bytes=VMEM_LIMIT),
        name="peer",
    )(h2t, rk, e1, cnt, e0, u_bf, vt_bf, x1, cond3, gf)


def kernel(x, c, w_ada, b_ada, g_norm1, w_in, b_in, conv_w, conv_b, g_conv_ln, b_conv_ln,
           g_v_ln, b_v_ln, w_spatial, b_spatial, w_out, b_out, g_norm2, w_query, sub_keys,
           expert_u, expert_v, g_final):
    B, S, D = x.shape
    depth = w_ada.shape[0]
    assert depth == 1, "the final rmsnorm is fused into the PEER call of the only layer"
    T = B * S
    row = lambda v: v.reshape(1, -1)
    grp = jnp.arange(D_CONV) // GROUP
    mg = jnp.where(grp[:, None] == grp[None, :], 1.0 / GROUP, 0.0).astype(BF16)
    c_pad = jnp.pad(c, ((0, 8 - B), (0, 0)))
    for l in range(depth):
        cond = _cond_call(c_pad, w_ada[l], row(b_ada[l]))
        cond3 = cond[:B].reshape(B, N_COND, D)
        bs = jnp.repeat(b_spatial[l].T, GROUP, axis=1)
        x1 = _mixer_call(x, cond3, row(g_norm1[l]), w_in[l].astype(BF16), row(b_in[l]),
                         conv_w[l], row(conv_b[l]), row(g_conv_ln[l]), row(b_conv_ln[l]),
                         row(g_v_ln[l]), row(b_v_ln[l]), w_spatial[l], bs, mg,
                         w_out[l].astype(BF16), row(b_out[l]))
        x1 = x1.reshape(T, D)
        keys = sub_keys[l].reshape(2 * PEER_HEADS, N_KEYS, D_HALF).astype(BF16)
        h2t, rk, e1, cnt, e0 = _prep_call(x1, cond3, row(g_norm2[l]), w_query[l].astype(BF16),
                                          keys, S // TB)
        vt = jnp.swapaxes(expert_v[l].astype(BF16).reshape(N_EXPERTS // EB, EB, D), 1, 2)
        x = _peer_call(h2t, rk, e1, cnt, e0, expert_u[l].astype(BF16), vt, x1, cond3,
                       row(g_final), S // TB).reshape(B, S, D)
    return x
```

```python
import jax
import jax.numpy as jnp
from jax import lax
from jax.experimental import pallas as pl
from jax.experimental.pallas import tpu as pltpu

F32 = jnp.float32
BF16 = jnp.bfloat16

D_MODEL = 1024
D_CONV = 512
D_GMLP = 512
GROUP = 64
N_HEADS_G = 8
CONV_WIDTH = 31
CHUNK = 128
N_PROJ = 2048
N_KEYS = 128
N_EXPERTS = N_KEYS * N_KEYS
PEER_HEADS = 8
D_HALF = 128
TOPK = 16
N_COND = 6
EPS = 1e-6
GELU_C1 = 0.7978845608028654
GELU_C2 = 0.044715 * GELU_C1

V7X_VMEM_BYTES = 64 * 1024 * 1024
VMEM_LIMIT = V7X_VMEM_BYTES * 3 // 4
SUBLANES = 8
LANES = 128
BF16_ROWS = 16

TS = 512
CONV_HALO = 32
CONV_ROWS = 128
TB = 512
EB = 2048


def _rms(x):
    return x * lax.rsqrt(jnp.mean(x * x, axis=-1, keepdims=True) + EPS)


def _split_dot(x, m):
    hi = x.astype(BF16)
    lo = (x - hi.astype(F32)).astype(BF16)
    return (jnp.dot(hi, m, preferred_element_type=F32)
            + jnp.dot(lo, m, preferred_element_type=F32))


def _group_ln(y, mg, g, b):
    mu = _split_dot(y, mg)
    d = y - mu
    var = _split_dot(d * d, mg)
    return d * lax.rsqrt(var + EPS) * g + b


def _cond_kernel(c_ref, w_ref, b_ref, o_ref):
    c = c_ref[...]
    s = c * jax.nn.sigmoid(c)
    o_ref[...] = jnp.dot(s, w_ref[...], preferred_element_type=F32,
                         precision=lax.Precision.HIGHEST) + b_ref[...]


def _cond_call(c_pad, w_ada, b_ada):
    n = w_ada.shape[1]
    blk = 1024
    return pl.pallas_call(
        _cond_kernel,
        grid=(n // blk,),
        in_specs=[pl.BlockSpec((8, D_MODEL), lambda j: (0, 0)),
                  pl.BlockSpec((D_MODEL, blk), lambda j: (0, j)),
                  pl.BlockSpec((1, blk), lambda j: (0, j))],
        out_specs=pl.BlockSpec((8, blk), lambda j: (0, j)),
        out_shape=jax.ShapeDtypeStruct((8, n), F32),
        compiler_params=pltpu.CompilerParams(
            dimension_semantics=("arbitrary",), vmem_limit_bytes=VMEM_LIMIT),
        name="cond",
    )(c_pad, w_ada, b_ada)


def _mixer_kernel(x_ref, cond_ref, g1_ref, win_ref, bin_ref, cw_ref, cb_ref,
                  gcl_ref, bcl_ref, gvl_ref, bvl_ref, ws_ref, bs_ref, mg_ref,
                  wout_ref, bout_ref, x1_ref, abuf, ybuf):
    s = pl.program_id(1)
    x = x_ref[0]
    cond = cond_ref[0]
    sh1, sc1, gt1 = cond[0:1], cond[1:2], cond[2:3]
    h = _rms(x) * g1_ref[...] * (1.0 + sc1) + sh1
    p = jnp.dot(h.astype(BF16), win_ref[...], preferred_element_type=F32) + bin_ref[...]

    a = p[:, :D_CONV] * jax.nn.sigmoid(p[:, D_CONV:2 * D_CONV])

    @pl.when(s == 0)
    def _():
        abuf[0:CONV_HALO, :] = jnp.zeros((CONV_HALO, D_CONV), F32)

    @pl.when(s > 0)
    def _():
        abuf[0:CONV_HALO, :] = abuf[TS:TS + CONV_HALO, :]

    abuf[CONV_HALO:CONV_HALO + TS, :] = a

    off = CONV_HALO - (CONV_WIDTH - 1)
    win_rows = CONV_ROWS + CONV_HALO
    for r in range(TS // CONV_ROWS):
        base = r * CONV_ROWS
        window = abuf[base:base + win_rows, :]
        shifted = {0: window}
        acc = jnp.zeros((CONV_ROWS, D_CONV), F32) + cb_ref[...]
        for k in range(CONV_WIDTH):
            res = (off + k) % SUBLANES
            lo = off + k - res
            if res not in shifted:
                shifted[res] = pltpu.roll(window, win_rows - res, axis=0)
            acc = acc + shifted[res][lo:lo + CONV_ROWS, :] * cw_ref[k:k + 1, :]
        ybuf[base:base + CONV_ROWS, :] = acc
    mg = mg_ref[...]
    ya = _group_ln(ybuf[...], mg, gcl_ref[...], bcl_ref[...])
    ya = ya * jax.nn.sigmoid(ya)

    z = jax.nn.gelu(p[:, 2 * D_CONV:])
    u = z[:, :D_GMLP]
    v = _group_ln(z[:, D_GMLP:], mg, gvl_ref[...], bvl_ref[...])
    row = lax.broadcasted_iota(jnp.int32, (CHUNK, CHUNK), 0)
    col = lax.broadcasted_iota(jnp.int32, (CHUNK, CHUNK), 1)
    lane_head = lax.broadcasted_iota(jnp.int32, (CHUNK, D_GMLP), 1) // GROUP
    w_heads = [jnp.where(row >= col, ws_ref[hh], 0.0).astype(BF16) for hh in range(N_HEADS_G)]
    yb_chunks = []
    for ch in range(TS // CHUNK):
        vc = v[ch * CHUNK:(ch + 1) * CHUNK, :]
        mixed = bs_ref[...]
        for hh in range(N_HEADS_G):
            vm = jnp.where(lane_head == hh, vc, 0.0).astype(BF16)
            mixed = mixed + jnp.dot(w_heads[hh], vm, preferred_element_type=F32)
        yb_chunks.append(u[ch * CHUNK:(ch + 1) * CHUNK, :] * mixed)
    yb = jnp.concatenate(yb_chunks, axis=0)

    y = (jnp.dot(ya.astype(BF16), wout_ref[0:D_CONV, :], preferred_element_type=F32)
         + jnp.dot(yb.astype(BF16), wout_ref[D_CONV:, :], preferred_element_type=F32)
         + bout_ref[...])
    x1_ref[0] = x + gt1 * y


def _mixer_call(x, cond3, g1, win, bin_, cw, cb, gcl, bcl, gvl, bvl, ws, bs, mg, wout, bout):
    B, S, D = x.shape
    full = lambda shape: pl.BlockSpec(shape, lambda b, s: (0,) * len(shape))
    return pl.pallas_call(
        _mixer_kernel,
        grid=(B, S // TS),
        in_specs=[pl.BlockSpec((1, TS, D), lambda b, s: (b, s, 0)),
                  pl.BlockSpec((1, N_COND, D), lambda b, s: (b, 0, 0)),
                  full((1, D)), full((D, N_PROJ)), full((1, N_PROJ)),
                  full((CONV_WIDTH, D_CONV)), full((1, D_CONV)),
                  full((1, D_CONV)), full((1, D_CONV)), full((1, D_GMLP)), full((1, D_GMLP)),
                  full((N_HEADS_G, CHUNK, CHUNK)), full((CHUNK, D_GMLP)),
                  full((D_CONV, D_CONV)), full((D, D)), full((1, D))],
        out_specs=pl.BlockSpec((1, TS, D), lambda b, s: (b, s, 0)),
        out_shape=jax.ShapeDtypeStruct((B, S, D), F32),
        scratch_shapes=[pltpu.VMEM((TS + CONV_HALO, D_CONV), F32),
                        pltpu.VMEM((TS, D_CONV), F32)],
        compiler_params=pltpu.CompilerParams(
            dimension_semantics=("arbitrary", "arbitrary"), vmem_limit_bytes=VMEM_LIMIT),
        name="mixer",
    )(x, cond3, g1, win, bin_, cw, cb, gcl, bcl, gvl, bvl, ws, bs, mg, wout, bout)


def _oddeven_merge(lo, hi, r):
    step = r * 2
    if step < hi - lo:
        yield from _oddeven_merge(lo, hi, step)
        yield from _oddeven_merge(lo + r, hi, step)
        yield from [(i, i + r) for i in range(lo + r, hi - r, step)]
    else:
        yield (lo, lo + r)


def _oddeven_merge_sort(lo, hi):
    if hi - lo >= 1:
        mid = lo + (hi - lo) // 2
        yield from _oddeven_merge_sort(lo, mid)
        yield from _oddeven_merge_sort(mid + 1, hi)
        yield from _oddeven_merge(lo, hi, 1)


SORT16 = tuple(_oddeven_merge_sort(0, TOPK - 1))
BITONIC16 = tuple((k, k + d) for d in (8, 4, 2, 1) for k in range(TOPK) if not k & d)


def _exchange(v, pairs):
    for i, j in pairs:
        v[i], v[j] = jnp.maximum(v[i], v[j]), jnp.minimum(v[i], v[j])
    return v


def _top16_sorted(s):
    v = _exchange([s[SUBLANES * k:SUBLANES * (k + 1), :] for k in range(TOPK)], SORT16)
    for shift in (4, 2, 1):
        v = _exchange([jnp.maximum(v[k], pltpu.roll(v[TOPK - 1 - k], shift, axis=0))
                       for k in range(TOPK)], BITONIC16)
    return v


def _top_values(work, n):
    vals = []
    for _ in range(n):
        m = jnp.max(work, axis=0, keepdims=True)
        vals.append(m)
        work = jnp.where(work == m, -jnp.inf, work)
    return vals


def _rows_of(vals, sublane):
    out = vals[SUBLANES - 1]
    for r in range(SUBLANES - 2, -1, -1):
        out = jnp.where(sublane == r, vals[r], out)
    return out


def _prep_kernel(x1_ref, cond_ref, g2_ref, wq_ref, keys_ref,
                 h2t_ref, rk_ref, e1_ref, cnt_ref, e0_ref, q_scr, s_scr):
    cond = cond_ref[0]
    sh2, sc2 = cond[3:4], cond[4:5]
    h2 = _rms(x1_ref[...]) * g2_ref[...] * (1.0 + sc2) + sh2
    h2t_ref[...] = h2.T.astype(BF16)
    q_scr[...] = jnp.dot(h2.astype(BF16), wq_ref[...], preferred_element_type=F32)

    def head(hh, carry):
        c0 = pl.multiple_of(hh * (2 * D_HALF), 2 * D_HALF)
        q0 = q_scr[:, pl.ds(c0, D_HALF)].astype(BF16)
        q1 = q_scr[:, pl.ds(c0 + D_HALF, D_HALF)].astype(BF16)
        nt = (((1,), (1,)), ((), ()))
        s_scr[0] = lax.dot_general(keys_ref[2 * hh], q0, nt, preferred_element_type=F32)
        s_scr[1] = lax.dot_general(keys_ref[2 * hh + 1], q1, nt, preferred_element_type=F32)

        def chunk(lc, carry2):
            lanes = pl.ds(pl.multiple_of(lc * LANES, LANES), LANES)
            s0 = s_scr[0, :, lanes]
            s1 = s_scr[1, :, lanes]
            a = _top16_sorted(s0)
            b = _top16_sorted(s1)
            sublane = lax.broadcasted_iota(jnp.int32, (SUBLANES, LANES), 0)
            a_lo, a_hi = _rows_of(a[:SUBLANES], sublane), _rows_of(a[SUBLANES:], sublane)
            b_hi = _rows_of(b[SUBLANES:], sublane)
            cand = [a_lo + b[0], a_hi + b[0]] + [a_lo + b[c] for c in range(1, SUBLANES)]
            cand.append(a[0] + b_hi)
            top = _top_values(jnp.concatenate(cand, axis=0), TOPK)
            tau = top[TOPK - 1]
            zsum = jnp.zeros_like(tau)
            for t in top:
                zsum = zsum + jnp.exp(t - top[0])
            rank1 = jnp.zeros(s1.shape, F32)
            count = jnp.zeros(s0.shape, F32)
            for c in range(TOPK):
                bc = jnp.concatenate([b[c]] * (N_KEYS // SUBLANES), axis=0)
                rank1 = jnp.where(bc > s1, float(c + 1), rank1)
                count = jnp.where(s0 + bc >= tau, float(c + 1), count)
            a0 = jnp.concatenate([a[0]] * (N_KEYS // SUBLANES), axis=0)
            b0 = jnp.concatenate([b[0]] * (N_KEYS // SUBLANES), axis=0)
            rk_ref[hh, :, lanes] = rank1.astype(BF16)
            e1_ref[hh, :, lanes] = (jnp.exp(s1 - b0) * (0.5 / zsum)).astype(BF16)
            cnt_ref[hh, :, lanes] = count
            e0_ref[hh, :, lanes] = jnp.exp(s0 - a0)
            return carry2

        lax.fori_loop(0, TB // LANES, chunk, 0, unroll=True)
        return carry

    lax.fori_loop(0, PEER_HEADS, head, 0, unroll=4)


def _prep_call(x1, cond3, g2, wq, keys, tiles_per_batch):
    T, D = x1.shape
    tok = lambda: pl.BlockSpec((PEER_HEADS, N_KEYS, TB), lambda i: (0, 0, i))
    sds = lambda dt: jax.ShapeDtypeStruct((PEER_HEADS, N_KEYS, T), dt)
    return pl.pallas_call(
        _prep_kernel,
        grid=(T // TB,),
        in_specs=[pl.BlockSpec((TB, D), lambda i: (i, 0)),
                  pl.BlockSpec((1, N_COND, D), lambda i: (i // tiles_per_batch, 0, 0)),
                  pl.BlockSpec((1, D), lambda i: (0, 0)),
                  pl.BlockSpec((D, 2 * PEER_HEADS * D_HALF), lambda i: (0, 0)),
                  pl.BlockSpec((2 * PEER_HEADS, N_KEYS, D_HALF), lambda i: (0, 0, 0))],
        out_specs=[pl.BlockSpec((D, TB), lambda i: (0, i)), tok(), tok(), tok(), tok()],
        out_shape=[jax.ShapeDtypeStruct((D, T), BF16),
                   sds(BF16), sds(BF16), sds(F32), sds(F32)],
        scratch_shapes=[pltpu.VMEM((TB, 2 * PEER_HEADS * D_HALF), F32),
                        pltpu.VMEM((2, N_KEYS, TB), F32)],
        compiler_params=pltpu.CompilerParams(
            dimension_semantics=("arbitrary",), vmem_limit_bytes=VMEM_LIMIT),
        name="prep",
    )(x1, cond3, g2, wq, keys)


def _peer_kernel(h2t_ref, rk_ref, e1_ref, cnt_ref, e0_ref, u_ref, vt_ref,
                 x1_ref, cond_ref, gf_ref, o_ref, rk_scr, e1_scr, a_scr, z_scr, acc_scr):
    j = pl.program_id(1)

    @pl.when(j == 0)
    def _():
        acc_scr[...] = jnp.zeros_like(acc_scr)
        for lc in range(TB // LANES):
            rk_scr[:, lc] = rk_ref[:, :, lc * LANES:(lc + 1) * LANES]
            e1_scr[:, lc] = e1_ref[:, :, lc * LANES:(lc + 1) * LANES]

    a_scr[...] = jnp.dot(u_ref[...], h2t_ref[...], preferred_element_type=F32)

    zero = jnp.zeros((BF16_ROWS, LANES), BF16)
    jtiles = N_KEYS // BF16_ROWS
    for ii in range(EB // N_KEYS):
        for lc in range(TB // LANES):
            lanes = slice(lc * LANES, (lc + 1) * LANES)
            g = [None] * jtiles
            for hh in range(PEER_HEADS):
                cnt = jnp.broadcast_to(cnt_ref[hh, ii:ii + 1, lanes], (BF16_ROWS, LANES)).astype(BF16)
                e0 = jnp.broadcast_to(e0_ref[hh, ii:ii + 1, lanes], (BF16_ROWS, LANES)).astype(BF16)
                for q in range(jtiles):
                    rk = rk_scr[hh, lc, q * BF16_ROWS:(q + 1) * BF16_ROWS, :]
                    e1 = e1_scr[hh, lc, q * BF16_ROWS:(q + 1) * BF16_ROWS, :]
                    term = jnp.minimum(jnp.maximum(cnt - rk, zero), e0) * e1
                    g[q] = term if g[q] is None else g[q] + term
            for q in range(jtiles):
                r0 = ii * N_KEYS + q * BF16_ROWS
                xb = a_scr[r0:r0 + BF16_ROWS, lanes].astype(BF16)
                th = jnp.tanh(xb * (GELU_C1 + GELU_C2 * (xb * xb)))
                z_scr[r0:r0 + BF16_ROWS, lanes] = (xb * th + xb) * g[q]
    acc_scr[...] += jnp.dot(vt_ref[0], z_scr[...], preferred_element_type=F32)

    @pl.when(j == pl.num_programs(1) - 1)
    def _():
        gt2 = cond_ref[0][5:6]
        x2 = x1_ref[...] + gt2 * acc_scr[...].T
        o_ref[...] = _rms(x2) * gf_ref[...]


def _peer_call(h2t, rk, e1, cnt, e0, u_bf, vt_bf, x1, cond3, gf, tiles_per_batch):
    T, D = x1.shape
    tok = lambda: pl.BlockSpec((PEER_HEADS, N_KEYS, TB), lambda i, j: (0, 0, i))
    rows = lambda: pl.BlockSpec((PEER_HEADS, EB // N_KEYS, TB), lambda i, j: (0, j, i))
    return pl.pallas_call(
        _peer_kernel,
        grid=(T // TB, N_EXPERTS // EB),
        in_specs=[pl.BlockSpec((D, TB), lambda i, j: (0, i)),
                  tok(), tok(), rows(), rows(),
                  pl.BlockSpec((EB, D), lambda i, j: (j, 0)),
                  pl.BlockSpec((1, D, EB), lambda i, j: (j, 0, 0)),
                  pl.BlockSpec((TB, D), lambda i, j: (i, 0)),
                  pl.BlockSpec((1, N_COND, D), lambda i, j: (i // tiles_per_batch, 0, 0)),
                  pl.BlockSpec((1, D), lambda i, j: (0, 0))],
        out_specs=pl.BlockSpec((TB, D), lambda i, j: (i, 0)),
        out_shape=jax.ShapeDtypeStruct((T, D), F32),
        scratch_shapes=[pltpu.VMEM((PEER_HEADS, TB // LANES, N_KEYS, LANES), BF16),
                        pltpu.VMEM((PEER_HEADS, TB // LANES, N_KEYS, LANES), BF16),
                        pltpu.VMEM((EB, TB), F32),
                        pltpu.VMEM((EB, TB), BF16),
                        pltpu.VMEM((D, TB), F32)],
        compiler_params=pltpu.CompilerParams(
            dimension_semantics=("arbitrary", "arbitrary"), vmem_limit_bytes=VMEM_LIMIT),
        name="peer",
    )(h2t, rk, e1, cnt, e0, u_bf, vt_bf, x1, cond3, gf)


def kernel(x, c, w_ada, b_ada, g_norm1, w_in, b_in, conv_w, conv_b, g_conv_ln, b_conv_ln,
           g_v_ln, b_v_ln, w_spatial, b_spatial, w_out, b_out, g_norm2, w_query, sub_keys,
           expert_u, expert_v, g_final):
    B, S, D = x.shape
    depth = w_ada.shape[0]
    assert depth == 1, "the final rmsnorm is fused into the PEER call of the only layer"
    T = B * S
    row = lambda v: v.reshape(1, -1)
    grp = jnp.arange(D_CONV) // GROUP
    mg = jnp.where(grp[:, None] == grp[None, :], 1.0 / GROUP, 0.0).astype(BF16)
    c_pad = jnp.pad(c, ((0, 8 - B), (0, 0)))
    for l in range(depth):
        cond = _cond_call(c_pad, w_ada[l], row(b_ada[l]))
        cond3 = cond[:B].reshape(B, N_COND, D)
        bs = jnp.repeat(b_spatial[l].T, GROUP, axis=1)
        x1 = _mixer_call(x, cond3, row(g_norm1[l]), w_in[l].astype(BF16), row(b_in[l]),
                         conv_w[l], row(conv_b[l]), row(g_conv_ln[l]), row(b_conv_ln[l]),
                         row(g_v_ln[l]), row(b_v_ln[l]), w_spatial[l], bs, mg,
                         w_out[l].astype(BF16), row(b_out[l]))
        x1 = x1.reshape(T, D)
        keys = sub_keys[l].reshape(2 * PEER_HEADS, N_KEYS, D_HALF).astype(BF16)
        h2t, rk, e1, cnt, e0 = _prep_call(x1, cond3, row(g_norm2[l]), w_query[l].astype(BF16),
                                          keys, S // TB)
        vt = jnp.swapaxes(expert_v[l].astype(BF16).reshape(N_EXPERTS // EB, EB, D), 1, 2)
        x = _peer_call(h2t, rk, e1, cnt, e0, expert_u[l].astype(BF16), vt, x1, cond3,
                       row(g_final), S // TB).reshape(B, S, D)
    return x
```
